```python
import math
import jax, jax.numpy as jnp
from jax import lax
import numpy as np

D_MODEL = 2048
BATCH = 1
SEQ = 8192
DEPTH = 2
DEC_BATCH = 128
DEC_SEQ = 8
PAST_LEN = 2048
PAGE_SIZE = 128

N_MIXERS = 2
N_GLA_LAYERS = (DEPTH + 1) // 2
N_SWA_LAYERS = DEPTH // 2

GLA_HEADS = 4
GLA_DK = D_MODEL // 2
GLA_DV = D_MODEL
GLA_DK_HEAD = GLA_DK // GLA_HEADS
GLA_DV_HEAD = GLA_DV // GLA_HEADS
GLA_GATE_RANK = 16
GLA_GATE_TAU = 16.0
GLA_CHUNK = 64
GLA_IN = 2 * GLA_DK + 2 * GLA_DV + GLA_GATE_RANK
RMS_EPS = 1e-6

HEAD_DIM = 128
SWA_KV_HEADS = D_MODEL // HEAD_DIM
DIL_GROUPS = ((128, 1), (512, 4), (2048, 16))
N_DIL = len(DIL_GROUPS)
SWA_Q_HEADS = N_DIL * SWA_KV_HEADS
SWA_MAX_WINDOW = max(w for w, _ in DIL_GROUPS)
SWA_Q_BLOCK = 128
SWA_IN = SWA_Q_HEADS * HEAD_DIM + 2 * SWA_KV_HEADS * HEAD_DIM

NUM_BUCKETS = 32
MAX_DISTANCE = 2048

N_EXPERTS = 16
N_EXPERT_GROUPS = 4
EXPERTS_PER_GROUP = N_EXPERTS // N_EXPERT_GROUPS
TOP_K = 2
D_EXPERT = 1024

DEEPNORM_ALPHA = (2 * DEPTH) ** 0.25
DEEPNORM_BETA = (8 * DEPTH) ** -0.25
LN_EPS = 1e-5

kernel_name = 'gla_dilated_swa_moe_deepnorm_step'


def layer_norm(x, g, b):
    xf = x.astype(jnp.float32)
    mu = xf.mean(-1, keepdims=True)
    var = jnp.square(xf - mu).mean(-1, keepdims=True)
    return ((xf - mu) * lax.rsqrt(var + LN_EPS) * g.astype(jnp.float32) + b.astype(jnp.float32)).astype(x.dtype)


def gla_chunked(q, k, v, log_a, s0, chunk):
    B, T, H, _ = q.shape
    n = T // chunk

    def to_chunks(t):
        return jnp.moveaxis(t.reshape(B, n, chunk, H, t.shape[-1]), 1, 0)

    causal = jnp.tril(jnp.ones((chunk, chunk), bool))[None, :, :, None, None]

    def step(s, inp):
        qc, kc, vc, ac = inp
        b = jnp.cumsum(ac, axis=1)
        o_inter = jnp.einsum('bchk,bhkv->bchv', qc * jnp.exp(b), s)
        decay = jnp.exp(jnp.where(causal, b[:, :, None] - b[:, None, :], -jnp.inf))
        att = jnp.einsum('bijhk,bjhk->bhij', qc[:, :, None] * decay, kc)
        o_intra = jnp.einsum('bhij,bjhv->bihv', att, vc)
        b_last = b[:, -1]
        s_new = jnp.exp(b_last)[..., None] * s + jnp.einsum('bjhk,bjhv->bhkv', kc * jnp.exp(b_last[:, None] - b), vc)
        return s_new, o_inter + o_intra

    s_fin, o = lax.scan(step, s0, (to_chunks(q), to_chunks(k), to_chunks(v), to_chunks(log_a)))
    return jnp.moveaxis(o, 0, 1).reshape(B, T, H, v.shape[-1]), s_fin


def gla_mixer(x, w_in, w_gate, b_gate, norm_w, w_out, s0):
    B, T, _ = x.shape
    proj = x @ w_in
    o1, o2, o3, o4 = GLA_DK, 2 * GLA_DK, 2 * GLA_DK + GLA_DV, 2 * GLA_DK + 2 * GLA_DV
    f32 = jnp.float32
    q = proj[..., :o1].astype(f32).reshape(B, T, GLA_HEADS, GLA_DK_HEAD) * GLA_DK_HEAD ** -0.5
    k = proj[..., o1:o2].astype(f32).reshape(B, T, GLA_HEADS, GLA_DK_HEAD)
    v = proj[..., o2:o3].astype(f32).reshape(B, T, GLA_HEADS, GLA_DV_HEAD)
    r = proj[..., o3:o4].astype(f32).reshape(B, T, GLA_HEADS, GLA_DV_HEAD)
    g_low = proj[..., o4:]
    log_a = jax.nn.log_sigmoid((g_low @ w_gate + b_gate).astype(f32)) / GLA_GATE_TAU
    log_a = log_a.reshape(B, T, GLA_HEADS, GLA_DK_HEAD)
    o, s_fin = gla_chunked(q, k, v, log_a, s0.astype(f32), math.gcd(T, GLA_CHUNK))
    o = o * lax.rsqrt(jnp.square(o).mean(-1, keepdims=True) + RMS_EPS) * norm_w.astype(f32)
    o = o * jax.nn.silu(r)
    y = o.reshape(B, T, GLA_DV).astype(x.dtype) @ w_out
    return y, s_fin


def t5_bucket(dist):
    max_exact = NUM_BUCKETS // 2
    d = jnp.maximum(dist.astype(jnp.float32), 1.0)
    large = max_exact + (jnp.log(d / max_exact) / math.log(MAX_DISTANCE / max_exact) * (NUM_BUCKETS - max_exact)).astype(jnp.int32)
    large = jnp.minimum(large, NUM_BUCKETS - 1)
    return jnp.where(dist < max_exact, dist, large)


def dilation_group_biases(rel_bias):
    out = []
    for g, (w, d) in enumerate(DIL_GROUPS):
        dist = d * jnp.arange(w // d + 1, dtype=jnp.int32)
        b = rel_bias[t5_bucket(dist)]
        out.append(b[:, g * SWA_KV_HEADS:(g + 1) * SWA_KV_HEADS].T.astype(jnp.float32))
    return out


def dilated_block(q_blk, k_all, v_all, q_pos, biases):
    outs, lses = [], []
    for g, (w, d) in enumerate(DIL_GROUPS):
        n_keys = w // d + 1
        key_idx = q_pos[:, None] - d * jnp.arange(n_keys, dtype=jnp.int32)[None, :]
        valid = key_idx >= 0
        safe = jnp.maximum(key_idx, 0)
        k_g = k_all[:, safe]
        v_g = v_all[:, safe]
        logits = jnp.einsum('bqhd,bqjhd->bhqj', q_blk[:, :, g], k_g, preferred_element_type=jnp.float32)
        logits = logits * HEAD_DIM ** -0.5 + biases[g][None, :, None, :]
        logits = jnp.where(valid[None, None], logits, -jnp.inf)
        m = logits.max(-1, keepdims=True)
        p = jnp.exp(logits - m)
        s = p.sum(-1, keepdims=True)
        outs.append(jnp.einsum('bhqj,bqjhd->bqhd', p / s, v_g.astype(jnp.float32)))
        lses.append(jnp.swapaxes((m + jnp.log(s))[..., 0], 1, 2))
    wts = jax.nn.softmax(jnp.stack(lses, 0), axis=0)
    return (wts[..., None] * jnp.stack(outs, 0)).sum(0)


def dilated_swa(q, k_all, v_all, q_offset, q_block, biases):
    B, T = q.shape[:2]
    nb = T // q_block
    qb = jnp.moveaxis(q.reshape(B, nb, q_block, N_DIL, SWA_KV_HEADS, HEAD_DIM), 1, 0)
    starts = q_offset + q_block * jnp.arange(nb, dtype=jnp.int32)

    def one(args):
        q_blk, st = args
        return dilated_block(q_blk, k_all, v_all, st + jnp.arange(q_block, dtype=jnp.int32), biases)

    ob = lax.map(one, (qb, starts))
    return jnp.moveaxis(ob, 0, 1).reshape(B, T, SWA_KV_HEADS * HEAD_DIM)


def swa_mixer(x, w_in, w_out, biases, cache_k, cache_v, q_block):
    B, T, _ = x.shape
    proj = x @ w_in
    nq, nkv = SWA_Q_HEADS * HEAD_DIM, SWA_KV_HEADS * HEAD_DIM
    q = proj[..., :nq].reshape(B, T, N_DIL, SWA_KV_HEADS, HEAD_DIM)
    k = proj[..., nq:nq + nkv].reshape(B, T, SWA_KV_HEADS, HEAD_DIM)
    v = proj[..., nq + nkv:].reshape(B, T, SWA_KV_HEADS, HEAD_DIM)
    if cache_k is None:
        k_all, v_all, off = k, v, 0
    else:
        k_all = jnp.concatenate([cache_k.astype(k.dtype), k], axis=1)
        v_all = jnp.concatenate([cache_v.astype(v.dtype), v], axis=1)
        off = cache_k.shape[1]
    o = dilated_swa(q, k_all, v_all, off, q_block, biases)
    y = o.astype(x.dtype) @ w_out
    return y, k, v


def moe_ffn(x, w_router, b_router, w_gate, w_up, w_down):
    shp = x.shape
    t = x.reshape(-1, shp[-1])
    scores = jax.nn.sigmoid(jnp.dot(t, w_router, preferred_element_type=jnp.float32))
    sel = (scores + b_router.astype(jnp.float32)).reshape(-1, N_EXPERT_GROUPS, EXPERTS_PER_GROUP)
    group_score = lax.top_k(sel, TOP_K)[0].sum(-1)
    g = jnp.argmax(group_score, axis=-1)
    in_group = jnp.take_along_axis(sel, g[:, None, None], axis=1)[:, 0]
    local = lax.top_k(in_group, TOP_K)[1]
    idx = g[:, None] * EXPERTS_PER_GROUP + local
    w = jnp.take_along_axis(scores, idx, axis=-1)
    w = w / w.sum(-1, keepdims=True)
    gates = (jax.nn.one_hot(idx, N_EXPERTS, dtype=jnp.float32) * w[..., None]).sum(1)
    h = jax.nn.silu(jnp.einsum('nd,edf->enf', t, w_gate)) * jnp.einsum('nd,edf->enf', t, w_up)
    h = h * gates.T[:, :, None].astype(h.dtype)
    y = jnp.einsum('enf,efd->nd', h, w_down)
    return y.reshape(shp)


def setup_inputs(seed: int = 0) -> dict:
    key = jax.random.key(seed)
    ks = jax.random.split(key, 24)
    f32 = jnp.float32

    def nrm(k, shape, scale):
        return jax.random.normal(k, shape, f32) * scale

    w_buf = min(SWA_MAX_WINDOW, PAST_LEN)
    return {
        'x_prompt': nrm(ks[0], (BATCH, SEQ, D_MODEL), 1.0),
        'x_sample': nrm(ks[1], (DEC_BATCH, DEC_SEQ, D_MODEL), 1.0),
        'state_gla': nrm(ks[2], (N_GLA_LAYERS, DEC_BATCH, GLA_HEADS, GLA_DK_HEAD, GLA_DV_HEAD), 1.0),
        'cache_swa_k': nrm(ks[3], (N_SWA_LAYERS, DEC_BATCH, w_buf, SWA_KV_HEADS, HEAD_DIM), 1.0),
        'cache_swa_v': nrm(ks[4], (N_SWA_LAYERS, DEC_BATCH, w_buf, SWA_KV_HEADS, HEAD_DIM), 1.0),
        'gla_w_in': nrm(ks[5], (N_GLA_LAYERS, D_MODEL, GLA_IN), D_MODEL ** -0.5),
        'gla_w_gate': nrm(ks[6], (N_GLA_LAYERS, GLA_GATE_RANK, GLA_DK), GLA_GATE_RANK ** -0.5),
        'gla_b_gate': nrm(ks[7], (N_GLA_LAYERS, GLA_DK), 0.1),
        'gla_norm_w': 1.0 + nrm(ks[8], (N_GLA_LAYERS, GLA_DV_HEAD), 0.01),
        'gla_w_out': nrm(ks[9], (N_GLA_LAYERS, GLA_DV, D_MODEL), GLA_DV ** -0.5 * DEEPNORM_BETA),
        'swa_w_in': nrm(ks[10], (N_SWA_LAYERS, D_MODEL, SWA_IN), D_MODEL ** -0.5),
        'swa_w_out': nrm(ks[11], (N_SWA_LAYERS, SWA_KV_HEADS * HEAD_DIM, D_MODEL), (SWA_KV_HEADS * HEAD_DIM) ** -0.5 * DEEPNORM_BETA),
        'rel_bias': nrm(ks[12], (NUM_BUCKETS, SWA_Q_HEADS), 0.1),
        'w_router': nrm(ks[13], (D_MODEL, N_EXPERTS), D_MODEL ** -0.5),
        'b_router': nrm(ks[14], (N_EXPERTS,), 0.01),
        'moe_w_gate': nrm(ks[15], (DEPTH, N_EXPERTS, D_MODEL, D_EXPERT), D_MODEL ** -0.5),
        'moe_w_up': nrm(ks[16], (DEPTH, N_EXPERTS, D_MODEL, D_EXPERT), D_MODEL ** -0.5),
        'moe_w_down': nrm(ks[17], (DEPTH, N_EXPERTS, D_EXPERT, D_MODEL), D_EXPERT ** -0.5 * DEEPNORM_BETA),
        'ln1_g': 1.0 + nrm(ks[18], (DEPTH, D_MODEL), 0.01),
        'ln1_b': nrm(ks[19], (DEPTH, D_MODEL), 0.01),
        'ln2_g': 1.0 + nrm(ks[20], (DEPTH, D_MODEL), 0.01),
        'ln2_b': nrm(ks[21], (DEPTH, D_MODEL), 0.01),
    }


def reference(x_prompt, x_sample, state_gla, cache_swa_k, cache_swa_v,
              gla_w_in, gla_w_gate, gla_b_gate, gla_norm_w, gla_w_out,
              swa_w_in, swa_w_out, rel_bias, w_router, b_router,
              moe_w_gate, moe_w_up, moe_w_down, ln1_g, ln1_b, ln2_g, ln2_b):
    xp, xs = x_prompt, x_sample
    biases = dilation_group_biases(rel_bias)
    gla_p, gla_s, kp, vp, ksm, vsm = [], [], [], [], [], []
    for i in range(DEPTH):
        j = i // N_MIXERS
        if i % N_MIXERS == 0:
            zero = jnp.zeros((xp.shape[0], GLA_HEADS, GLA_DK_HEAD, GLA_DV_HEAD), jnp.float32)
            mp, st_p = gla_mixer(xp, gla_w_in[j], gla_w_gate[j], gla_b_gate[j], gla_norm_w[j], gla_w_out[j], zero)
            ms, st_s = gla_mixer(xs, gla_w_in[j], gla_w_gate[j], gla_b_gate[j], gla_norm_w[j], gla_w_out[j], state_gla[j])
            gla_p.append(st_p.astype(state_gla.dtype))
            gla_s.append(st_s.astype(state_gla.dtype))
        else:
            mp, k_p, v_p = swa_mixer(xp, swa_w_in[j], swa_w_out[j], biases, None, None, math.gcd(xp.shape[1], SWA_Q_BLOCK))
            ms, k_s, v_s = swa_mixer(xs, swa_w_in[j], swa_w_out[j], biases, cache_swa_k[j], cache_swa_v[j], 1)
            keep = min(SWA_MAX_WINDOW, xp.shape[1])
            kp.append(k_p[:, xp.shape[1] - keep:])
            vp.append(v_p[:, xp.shape[1] - keep:])
            ksm.append(k_s)
            vsm.append(v_s)
        xp = layer_norm(DEEPNORM_ALPHA * xp + mp, ln1_g[i], ln1_b[i])
        xs = layer_norm(DEEPNORM_ALPHA * xs + ms, ln1_g[i], ln1_b[i])
        xp = layer_norm(DEEPNORM_ALPHA * xp + moe_ffn(xp, w_router, b_router, moe_w_gate[i], moe_w_up[i], moe_w_down[i]), ln2_g[i], ln2_b[i])
        xs = layer_norm(DEEPNORM_ALPHA * xs + moe_ffn(xs, w_router, b_router, moe_w_gate[i], moe_w_up[i], moe_w_down[i]), ln2_g[i], ln2_b[i])
    return (xp, xs, jnp.stack(gla_p), jnp.stack(gla_s), jnp.stack(kp), jnp.stack(vp), jnp.stack(ksm), jnp.stack(vsm))
```

```python
import functools
import math

import numpy as np
import jax
import jax.numpy as jnp
from jax import lax
from jax.experimental import pallas as pl
from jax.experimental.pallas import tpu as pltpu

F32 = jnp.float32
BF16 = jnp.bfloat16
I32 = jnp.int32

D_MODEL = 2048
DEPTH = 2
GLA_HEADS = 4
GLA_DK_HEAD = 256
GLA_DV_HEAD = 512
GLA_DK = GLA_HEADS * GLA_DK_HEAD
GLA_DV = GLA_HEADS * GLA_DV_HEAD
GLA_GATE_RANK = 16
GLA_GATE_TAU = 16.0
GLA_CHUNK = 64
GLA_MAIN = 2 * GLA_DK + 2 * GLA_DV
RMS_EPS = 1e-6
HEAD_DIM = 128
SWA_HEADS = 16
DIL_GROUPS = ((128, 1), (512, 4), (2048, 16))
N_DIL = len(DIL_GROUPS)
SWA_NQ = N_DIL * SWA_HEADS * HEAD_DIM
SWA_NKV = SWA_HEADS * HEAD_DIM
SWA_MAX_WINDOW = 2048
NUM_BUCKETS = 32
MAX_DISTANCE = 2048
N_EXPERTS = 16
N_EXPERT_GROUPS = 4
EXPERTS_PER_GROUP = 4
D_EXPERT = 1024
DEEPNORM_ALPHA = (2 * DEPTH) ** 0.25
LN_EPS = 1e-5

LANES = 128
SUBLANES = 8
VMEM_LIMIT_BYTES = 56 * 1024 * 1024
NEG_BIG = -1e30


PROJ_TM = 1024
LN_TM = 256


def _pick_tile(n, preferred):
    t = preferred
    while t > SUBLANES and n % t:
        t //= 2
    assert n % t == 0, (n, t)
    return t


def _params(*sem):
    return pltpu.CompilerParams(dimension_semantics=sem, vmem_limit_bytes=VMEM_LIMIT_BYTES)


def _dot(a, b):
    return jnp.dot(a, b, preferred_element_type=F32)


def _dot_nt(a, b):
    return lax.dot_general(a, b, (((1,), (1,)), ((), ())), preferred_element_type=F32)


def _dot_tn(a, b):
    return lax.dot_general(a, b, (((0,), (0,)), ((), ())), preferred_element_type=F32)


def _split3(a):
    hi = a.astype(BF16)
    r1 = a - hi.astype(F32)
    mid = r1.astype(BF16)
    lo = (r1 - mid.astype(F32)).astype(BF16)
    return hi, mid, lo


def _mm_kernel(x_ref, w_ref, o_ref, xb_ref):
    @pl.when(pl.program_id(1) == 0)
    def _():
        xb_ref[...] = x_ref[...].astype(BF16)

    o_ref[...] = _dot(xb_ref[...], w_ref[...].astype(BF16)).astype(o_ref.dtype)


def _matmul(x, w, n_out, tm, tn):
    m, k = x.shape
    assert m % tm == 0 and n_out % tn == 0 and w.shape[0] == k
    return pl.pallas_call(
        _mm_kernel,
        grid=(m // tm, n_out // tn),
        in_specs=[pl.BlockSpec((tm, k), lambda i, j: (i, 0)),
                  pl.BlockSpec((k, tn), lambda i, j: (0, j))],
        out_specs=pl.BlockSpec((tm, tn), lambda i, j: (i, j)),
        out_shape=jax.ShapeDtypeStruct((m, n_out), F32),
        scratch_shapes=[pltpu.VMEM((tm, k), BF16)],
        compiler_params=_params("arbitrary", "arbitrary"),
        name="dense_proj",
    )(x, w)


def _gla_kernel(q_ref, k_ref, v_ref, r_ref, gl_ref, wg_ref, bg_ref, nw_ref, s0_ref,
                o_ref, sf_ref, state, bsc, *, chunk, n_chunks, mxu_intra):
    c_sz = chunk
    t = pl.program_id(2)

    @pl.when(t == 0)
    def _():
        state[...] = s0_ref[...]

    rowi = lax.broadcasted_iota(I32, (c_sz, GLA_DK_HEAD), 0)
    tri = (lax.broadcasted_iota(I32, (c_sz, c_sz), 0) >= lax.broadcasted_iota(I32, (c_sz, c_sz), 1)).astype(BF16)
    ones = jnp.ones((c_sz, LANES), BF16)
    att_lanes = max(c_sz, LANES)
    lanei = lax.broadcasted_iota(I32, (c_sz, att_lanes), 1)
    wg = wg_ref[...].astype(BF16)

    def one_chunk(c):
        rows = pl.ds(pl.multiple_of(c * c_sz, c_sz), c_sz)
        q = q_ref[rows, :] * (GLA_DK_HEAD ** -0.5)
        k = k_ref[rows, :]
        v = v_ref[rows, :]
        z = _dot(gl_ref[rows, :].astype(BF16), wg) + bg_ref[...]
        la = (jnp.minimum(z, 0.0) - jnp.log(1.0 + jnp.exp(-jnp.abs(z)))) * (1.0 / GLA_GATE_TAU)
        l_hi, l_mid, l_lo = _split3(la)
        b = _dot(tri, l_hi) + (_dot(tri, l_mid) + _dot(tri, l_lo))
        dcol = _dot_tn(l_hi, ones) + (_dot_tn(l_mid, ones) + _dot_tn(l_lo, ones))
        b_last = b[c_sz - 1:c_sz, :]

        s_old = state[...]
        o = _dot((q * jnp.exp(b)).astype(BF16), s_old.astype(BF16))

        bsc[...] = b
        qk_base = c * c_sz

        def col_of(j):
            kj = k_ref[pl.ds(qk_base + j, 1), :]
            bj = bsc[pl.ds(j, 1), :]
            e = jnp.exp(jnp.where(rowi >= j, b - bj, -jnp.inf))
            return jnp.sum(q * e * kj, axis=-1, keepdims=True)

        if mxu_intra:
            def jbody(j, att):
                return jnp.where(lanei == j, col_of(j), att)
            att = lax.fori_loop(0, c_sz, jbody, jnp.zeros((c_sz, att_lanes), F32))
            o = o + _dot(att[:, :c_sz].astype(BF16), v.astype(BF16))
        else:
            def jbody(j, acc):
                vj = v_ref[pl.ds(qk_base + j, 1), :]
                return acc + col_of(j) * vj
            o = lax.fori_loop(0, c_sz, jbody, o)

        kt = (k * jnp.exp(b_last - b)).astype(BF16)
        decay = jnp.exp(dcol)
        ds = _dot_tn(kt, v.astype(BF16))
        state[...] = s_old * jnp.tile(decay, (1, GLA_DV_HEAD // LANES)) + ds

        o = o * lax.rsqrt(jnp.mean(o * o, axis=-1, keepdims=True) + RMS_EPS) * nw_ref[...]
        r = r_ref[rows, :]
        o_ref[rows, :] = o * (r / (1.0 + jnp.exp(-r)))

    if n_chunks == 1:
        one_chunk(0)
    else:
        def body(c, carry):
            one_chunk(c)
            return carry
        lax.fori_loop(0, n_chunks, body, 0)

    @pl.when(t == pl.num_programs(2) - 1)
    def _():
        sf_ref[...] = state[...]


def _gla(proj, glow, w_gate_pad, b_gate, norm_w, s0, *, n_seq, seq_len, row_start, rows_per_step, chunk,
         mxu_intra):
    r_sz = rows_per_step
    assert seq_len % r_sz == 0 and r_sz % chunk == 0 and row_start % r_sz == 0
    steps = seq_len // r_sz
    rb0 = row_start // r_sz
    kq, kv = GLA_DK // GLA_DK_HEAD, GLA_DK // GLA_DV_HEAD

    def rowblk(b, t):
        return rb0 + b * steps + t

    kern = functools.partial(_gla_kernel, chunk=chunk, n_chunks=r_sz // chunk, mxu_intra=mxu_intra)
    return pl.pallas_call(
        kern,
        grid=(n_seq, GLA_HEADS, steps),
        in_specs=[
            pl.BlockSpec((r_sz, GLA_DK_HEAD), lambda b, h, t: (rowblk(b, t), h)),
            pl.BlockSpec((r_sz, GLA_DK_HEAD), lambda b, h, t: (rowblk(b, t), kq + h)),
            pl.BlockSpec((r_sz, GLA_DV_HEAD), lambda b, h, t: (rowblk(b, t), 2 * kv + h)),
            pl.BlockSpec((r_sz, GLA_DV_HEAD), lambda b, h, t: (rowblk(b, t), 2 * kv + GLA_HEADS + h)),
            pl.BlockSpec((r_sz, LANES), lambda b, h, t: (rowblk(b, t), 0)),
            pl.BlockSpec((LANES, GLA_DK_HEAD), lambda b, h, t: (0, h)),
            pl.BlockSpec((1, GLA_DK_HEAD), lambda b, h, t: (0, h)),
            pl.BlockSpec((1, GLA_DV_HEAD), lambda b, h, t: (0, 0)),
            pl.BlockSpec((None, None, GLA_DK_HEAD, GLA_DV_HEAD), lambda b, h, t: (b, h, 0, 0)),
        ],
        out_specs=[
            pl.BlockSpec((r_sz, GLA_DV_HEAD), lambda b, h, t: (b * steps + t, h)),
            pl.BlockSpec((None, None, GLA_DK_HEAD, GLA_DV_HEAD), lambda b, h, t: (b, h, 0, 0)),
        ],
        out_shape=[jax.ShapeDtypeStruct((n_seq * seq_len, GLA_DV), F32),
                   jax.ShapeDtypeStruct((n_seq, GLA_HEADS, GLA_DK_HEAD, GLA_DV_HEAD), F32)],
        scratch_shapes=[pltpu.VMEM((GLA_DK_HEAD, GLA_DV_HEAD), F32),
                        pltpu.VMEM((chunk, GLA_DK_HEAD), F32)],
        compiler_params=_params("arbitrary", "arbitrary", "arbitrary"),
        name="gla_chunked",
    )(proj, proj, proj, proj, glow, w_gate_pad, b_gate, norm_w, s0)


def _layer_norm(x, g, b):
    mu = jnp.mean(x, axis=-1, keepdims=True)
    xc = x - mu
    var = jnp.mean(xc * xc, axis=-1, keepdims=True)
    return xc * lax.rsqrt(var + LN_EPS) * g + b


def _route(x, wr_ref, br_ref, idx_ref, gate_ref):
    tm = x.shape[0]
    xh, xm, xl = _split3(x)
    wh, wm, wl = _split3(wr_ref[...])
    logits = _dot(xh, wh) + ((_dot(xh, wm) + _dot(xm, wh)) + ((_dot(xm, wm) + _dot(xh, wl)) + _dot(xl, wh)))
    scores = 1.0 / (1.0 + jnp.exp(-logits))
    lane = lax.broadcasted_iota(I32, (tm, LANES), 1).astype(F32)
    sel = scores + br_ref[...]
    big = float(LANES)
    best = None
    for g in range(N_EXPERT_GROUPS):
        in_g = (lane >= g * EXPERTS_PER_GROUP) & (lane < (g + 1) * EXPERTS_PER_GROUP)
        vg = jnp.where(in_g, sel, -jnp.inf)
        m1 = jnp.max(vg, axis=-1, keepdims=True)
        i1 = jnp.min(jnp.where(vg == m1, lane, big), axis=-1, keepdims=True)
        vg2 = jnp.where(lane == i1, -jnp.inf, vg)
        m2 = jnp.max(vg2, axis=-1, keepdims=True)
        i2 = jnp.min(jnp.where(vg2 == m2, lane, big), axis=-1, keepdims=True)
        gs = m1 + m2
        if best is None:
            best, b1, b2 = gs, i1, i2
        else:
            better = gs > best
            best = jnp.where(better, gs, best)
            b1 = jnp.where(better, i1, b1)
            b2 = jnp.where(better, i2, b2)
    w1 = jnp.sum(jnp.where(lane == b1, scores, 0.0), axis=-1, keepdims=True)
    w2 = jnp.sum(jnp.where(lane == b2, scores, 0.0), axis=-1, keepdims=True)
    den = w1 + w2
    idx_ref[...] = jnp.where(lane == 0.0, b1, jnp.where(lane == 1.0, b2, 0.0)).astype(I32)
    gate_ref[...] = jnp.where(lane == 0.0, w1 / den, jnp.where(lane == 1.0, w2 / den, 0.0))


def _ln_route_kernel(x_ref, y_ref, g_ref, b_ref, wr_ref, br_ref, o_ref, idx_ref, gate_ref):
    x1 = _layer_norm(DEEPNORM_ALPHA * x_ref[...] + y_ref[...], g_ref[...], b_ref[...])
    o_ref[...] = x1
    _route(x1, wr_ref, br_ref, idx_ref, gate_ref)


def _ln_route(x, y, g, b, wr_pad, br_pad, tm):
    n, d = x.shape
    row = lambda i: (i, 0)
    fixed = lambda i: (0, 0)
    return pl.pallas_call(
        _ln_route_kernel,
        grid=(n // tm,),
        in_specs=[pl.BlockSpec((tm, d), row), pl.BlockSpec((tm, d), row),
                  pl.BlockSpec((1, d), fixed), pl.BlockSpec((1, d), fixed),
                  pl.BlockSpec((d, LANES), fixed), pl.BlockSpec((1, LANES), fixed)],
        out_specs=[pl.BlockSpec((tm, d), row), pl.BlockSpec((tm, LANES), row), pl.BlockSpec((tm, LANES), row)],
        out_shape=[jax.ShapeDtypeStruct((n, d), F32), jax.ShapeDtypeStruct((n, LANES), I32),
                   jax.ShapeDtypeStruct((n, LANES), F32)],
        compiler_params=_params("arbitrary"),
        name="deepnorm_ln_router",
    )(x, y, g, b, wr_pad, br_pad)


def _gather_kernel(src_ref, nu_ref, x_hbm, o_ref, buf, sem, *, tm):
    i = pl.program_id(0)

    @pl.when(i < nu_ref[0])
    def _():
        base = i * tm

        def row_copy(r):
            return pltpu.make_async_copy(x_hbm.at[pl.ds(src_ref[base + r], 1), :], buf.at[pl.ds(r, 1), :], sem)

        def issue(r, c):
            row_copy(r).start()
            return c

        def drain(r, c):
            row_copy(r).wait()
            return c

        lax.fori_loop(0, tm, issue, 0)
        lax.fori_loop(0, tm, drain, 0)
        o_ref[...] = buf[...].astype(BF16)

    @pl.when(i >= nu_ref[0])
    def _():
        o_ref[...] = jnp.zeros(o_ref.shape, o_ref.dtype)


def _gather_rows(x, src_row, n_used, tm):
    n_tiles = src_row.shape[0] // tm
    d = x.shape[1]
    return pl.pallas_call(
        functools.partial(_gather_kernel, tm=tm),
        grid_spec=pltpu.PrefetchScalarGridSpec(
            num_scalar_prefetch=2,
            grid=(n_tiles,),
            in_specs=[pl.BlockSpec(memory_space=pl.ANY)],
            out_specs=pl.BlockSpec((tm, d), lambda i, src, nu: (i, 0)),
            scratch_shapes=[pltpu.VMEM((tm, d), F32), pltpu.SemaphoreType.DMA(())],
        ),
        out_shape=jax.ShapeDtypeStruct((n_tiles * tm, d), BF16),
        compiler_params=_params("arbitrary"),
        name="moe_gather",
    )(src_row, n_used, x)


def _gmm1_kernel(te_ref, tb_ref, ch_ref, nu_ref, x_ref, wg_ref, wu_ref, h_ref, wgb, wub):
    i = pl.program_id(1)

    @pl.when(i < nu_ref[0])
    def _():
        @pl.when(ch_ref[i] == 1)
        def _():
            wgb[...] = wg_ref[...].astype(BF16)
            wub[...] = wu_ref[...].astype(BF16)

        x = x_ref[...]
        a = _dot(x, wgb[...])
        u = _dot(x, wub[...])
        h_ref[...] = ((a / (1.0 + jnp.exp(-a))) * u).astype(BF16)

    @pl.when(i >= nu_ref[0])
    def _():
        h_ref[...] = jnp.zeros(h_ref.shape, h_ref.dtype)


def _gmm1(xs, w_gate, w_up, layer, meta, tm, tf):
    te, tb, ch, nu = meta
    n_tiles = xs.shape[0] // tm
    d = xs.shape[1]
    f = w_gate.shape[-1]
    wspec = pl.BlockSpec((None, None, d, tf), lambda j, i, te, tb, ch, nu: (layer, te[i], 0, j))
    return pl.pallas_call(
        _gmm1_kernel,
        grid_spec=pltpu.PrefetchScalarGridSpec(
            num_scalar_prefetch=4,
            grid=(f // tf, n_tiles),
            in_specs=[pl.BlockSpec((tm, d), lambda j, i, te, tb, ch, nu: (tb[i], 0)), wspec, wspec],
            out_specs=pl.BlockSpec((tm, tf), lambda j, i, te, tb, ch, nu: (i, j)),
            scratch_shapes=[pltpu.VMEM((d, tf), BF16), pltpu.VMEM((d, tf), BF16)],
        ),
        out_shape=jax.ShapeDtypeStruct((n_tiles * tm, f), BF16),
        compiler_params=_params("arbitrary", "arbitrary"),
        name="moe_gate_up",
    )(te, tb, ch, nu, xs, w_gate, w_up)


def _gmm2_kernel(te_ref, tb_ref, ch_ref, nu_ref, h_ref, wd_ref, y_ref, wdb):
    i = pl.program_id(1)

    @pl.when(i < nu_ref[0])
    def _():
        @pl.when(ch_ref[i] == 1)
        def _():
            wdb[...] = wd_ref[...].astype(BF16)

        y_ref[...] = _dot(h_ref[...], wdb[...])

    @pl.when(i >= nu_ref[0])
    def _():
        y_ref[...] = jnp.zeros(y_ref.shape, y_ref.dtype)


def _gmm2(h, w_down, layer, meta, tm, tn):
    te, tb, ch, nu = meta
    n_tiles = h.shape[0] // tm
    f = h.shape[1]
    d = w_down.shape[-1]
    return pl.pallas_call(
        _gmm2_kernel,
        grid_spec=pltpu.PrefetchScalarGridSpec(
            num_scalar_prefetch=4,
            grid=(d // tn, n_tiles),
            in_specs=[pl.BlockSpec((tm, f), lambda j, i, te, tb, ch, nu: (tb[i], 0)),
                      pl.BlockSpec((None, None, f, tn), lambda j, i, te, tb, ch, nu: (layer, te[i], 0, j))],
            out_specs=pl.BlockSpec((tm, tn), lambda j, i, te, tb, ch, nu: (i, j)),
            scratch_shapes=[pltpu.VMEM((f, tn), BF16)],
        ),
        out_shape=jax.ShapeDtypeStruct((n_tiles * tm, d), F32),
        compiler_params=_params("arbitrary", "arbitrary"),
        name="moe_down",
    )(te, tb, ch, nu, h, w_down)


def _combine_ln_kernel(pos_ref, y_hbm, x_ref, gate_ref, g_ref, b_ref, *rest, tm, n_split_tiles):
    if n_split_tiles is None:
        (o_ref, y0, y1, sem) = rest
    else:
        (op_ref, os_ref, y0, y1, sem) = rest
    i = pl.program_id(0)
    base = i * tm

    def row_copy(r, k, buf):
        return pltpu.make_async_copy(y_hbm.at[pl.ds(pos_ref[2 * (base + r) + k], 1), :], buf.at[pl.ds(r, 1), :], sem)

    def issue(r, c):
        row_copy(r, 0, y0).start()
        row_copy(r, 1, y1).start()
        return c

    def drain(r, c):
        row_copy(r, 0, y0).wait()
        row_copy(r, 1, y1).wait()
        return c

    lax.fori_loop(0, tm, issue, 0)
    lax.fori_loop(0, tm, drain, 0)
    gate = gate_ref[...]
    moe = gate[:, 0:1] * y0[...] + gate[:, 1:2] * y1[...]
    out = _layer_norm(DEEPNORM_ALPHA * x_ref[...] + moe, g_ref[...], b_ref[...])
    if n_split_tiles is None:
        o_ref[...] = out
    else:
        @pl.when(i < n_split_tiles)
        def _():
            op_ref[...] = out

        @pl.when(i >= n_split_tiles)
        def _():
            os_ref[...] = out


def _combine_ln(y_sorted, pos, x, gate, g, b, tm, n_split=None):
    n, d = x.shape
    row = lambda i, pos: (i, 0)
    fixed = lambda i, pos: (0, 0)
    if n_split is None:
        nst = None
        out_specs = pl.BlockSpec((tm, d), row)
        out_shape = jax.ShapeDtypeStruct((n, d), F32)
    else:
        nst = n_split // tm
        out_specs = [pl.BlockSpec((tm, d), lambda i, pos: (jnp.minimum(i, nst - 1), 0)),
                     pl.BlockSpec((tm, d), lambda i, pos: (jnp.maximum(i - nst, 0), 0))]
        out_shape = [jax.ShapeDtypeStruct((n_split, d), F32), jax.ShapeDtypeStruct((n - n_split, d), F32)]
    return pl.pallas_call(
        functools.partial(_combine_ln_kernel, tm=tm, n_split_tiles=nst),
        grid_spec=pltpu.PrefetchScalarGridSpec(
            num_scalar_prefetch=1,
            grid=(n // tm,),
            in_specs=[pl.BlockSpec(memory_space=pl.ANY), pl.BlockSpec((tm, d), row),
                      pl.BlockSpec((tm, LANES), row), pl.BlockSpec((1, d), fixed), pl.BlockSpec((1, d), fixed)],
            out_specs=out_specs,
            scratch_shapes=[pltpu.VMEM((tm, d), F32), pltpu.VMEM((tm, d), F32), pltpu.SemaphoreType.DMA(())],
        ),
        out_shape=out_shape,
        compiler_params=_params("arbitrary"),
        name="moe_combine_ln",
    )(pos, y_sorted, x, gate, g, b)


def _dispatch_meta(idx, tm, n_tiles):
    n = idx.shape[0]
    e_flat = idx[:, :2].reshape(-1)
    onehot = (e_flat[:, None] == jnp.arange(N_EXPERTS, dtype=I32)[None, :]).astype(I32)
    rank = jnp.take_along_axis(jnp.cumsum(onehot, axis=0), e_flat[:, None], axis=1)[:, 0] - 1
    counts = onehot.sum(axis=0)
    ptiles = (counts + tm - 1) // tm
    tile_end = jnp.cumsum(ptiles)
    n_used = tile_end[-1]
    row_off = (tile_end - ptiles) * tm
    pos = row_off[e_flat] + rank
    tok = jnp.arange(2 * n, dtype=I32) // 2
    src_row = jnp.zeros((n_tiles * tm,), I32).at[pos].set(tok)
    tile_id = jnp.minimum(jnp.arange(n_tiles, dtype=I32), n_used - 1)
    tile_expert = jnp.searchsorted(tile_end, tile_id, side="right").astype(I32)
    changed = jnp.concatenate([jnp.ones((1,), I32), (tile_expert[1:] != tile_expert[:-1]).astype(I32)])
    nu = n_used.reshape(1).astype(I32)
    return src_row, pos.astype(I32), (tile_expert, tile_id.astype(I32), changed, nu)


MOE_TM = 256
MOE_TF = 512
MOE_TN = 512


def _moe_ln(x1, idx, gate, w_gate, w_up, w_down, layer, g, b, n_split=None):
    n = x1.shape[0]
    n_tiles = -(-2 * n // MOE_TM) + N_EXPERTS
    src_row, pos, meta = _dispatch_meta(idx, MOE_TM, n_tiles)
    xs = _gather_rows(x1, src_row, meta[3], MOE_TM)
    h = _gmm1(xs, w_gate, w_up, layer, meta, MOE_TM, MOE_TF)
    y = _gmm2(h, w_down, layer, meta, MOE_TM, MOE_TN)
    tm = _pick_tile(n if n_split is None else math.gcd(n, n_split), LN_TM)
    return _combine_ln(y, pos, x1, gate, g, b, tm, n_split)


SWA_BLK = 128
SWA_SUPER = SWA_BLK * 16
SWA_SCALE = HEAD_DIM ** -0.5


def _t5_bucket(dist):
    max_exact = NUM_BUCKETS // 2
    d = jnp.maximum(dist.astype(F32), 1.0)
    large = max_exact + (jnp.log(d / max_exact) / math.log(MAX_DISTANCE / max_exact)
                         * (NUM_BUCKETS - max_exact)).astype(I32)
    large = jnp.minimum(large, NUM_BUCKETS - 1)
    return jnp.where(dist < max_exact, dist, large)


def _group_biases(rel_bias):
    out = []
    for g, (w, d) in enumerate(DIL_GROUPS):
        dist = d * jnp.arange(w // d + 1, dtype=I32)
        b = rel_bias[_t5_bucket(dist)]
        out.append(b[:, g * SWA_HEADS:(g + 1) * SWA_HEADS].T.astype(F32))
    return out


def _prompt_bias_tables(biases):
    qi = np.arange(SWA_BLK)[:, None]
    kj = np.arange(2 * SWA_BLK)[None, :]
    j = qi + SWA_BLK - kj
    valid = (j >= 0) & (j <= SWA_BLK)
    jc = np.clip(j, 0, SWA_BLK)
    return jnp.concatenate([jnp.where(valid[None], b[:, jc], NEG_BIG) for b in biases], axis=0)


def _swa_prompt_kernel(q0_ref, q1_ref, q2_ref, kc_ref, kp_ref, vc_ref, vp_ref, t0_ref, t1_ref, t2_ref,
                       o_ref, kk, vv, og, ls):
    s = pl.program_id(1)
    sb = SWA_SUPER
    kk[0:sb, :] = kp_ref[...]
    kk[sb:2 * sb, :] = kc_ref[...]
    vv[0:sb, :] = vp_ref[...]
    vv[sb:2 * sb, :] = vc_ref[...]
    col = lax.broadcasted_iota(I32, (SWA_BLK, 2 * SWA_BLK), 1)
    before_start = col < SWA_BLK

    for g, (_, d) in enumerate(DIL_GROUPS):
        q_ref = (q0_ref, q1_ref, q2_ref)[g]
        tab = (t0_ref, t1_ref, t2_ref)[g][...]
        n_mb = sb // (SWA_BLK * d)

        def block(idx, carry, q_ref=q_ref, tab=tab, d=d, n_mb=n_mb, g=g):
            r = idx // n_mb
            mb = idx % n_mb
            start = r + d * SWA_BLK * mb
            kstart = sb + start - d * SWA_BLK
            if d == 1:
                rows_q, rows_k = pl.ds(start, SWA_BLK), pl.ds(kstart, 2 * SWA_BLK)
            else:
                rows_q, rows_k = pl.ds(start, SWA_BLK, stride=d), pl.ds(kstart, 2 * SWA_BLK, stride=d)
            q = q_ref[rows_q, :].astype(BF16)
            k = kk[rows_k, :].astype(BF16)
            v = vv[rows_k, :].astype(BF16)
            lg = _dot_nt(q, k) * SWA_SCALE + tab
            no_prev = jnp.logical_and(s == 0, mb == 0)
            lg = jnp.where(jnp.logical_and(no_prev, before_start), NEG_BIG, lg)
            m = jnp.max(lg, axis=-1, keepdims=True)
            p = jnp.exp(lg - m)
            l = jnp.sum(p, axis=-1, keepdims=True)
            og[g, rows_q, :] = _dot(p.astype(BF16), v) / l
            ls[g, rows_q, :] = jnp.broadcast_to(m + jnp.log(l), (SWA_BLK, HEAD_DIM))
            return carry

        lax.fori_loop(0, sb // SWA_BLK, block, 0)

    l0, l1, l2 = ls[0], ls[1], ls[2]
    mx = jnp.maximum(jnp.maximum(l0, l1), l2)
    e0, e1, e2 = jnp.exp(l0 - mx), jnp.exp(l1 - mx), jnp.exp(l2 - mx)
    o_ref[...] = (e0 * og[0] + e1 * og[1] + e2 * og[2]) / (e0 + e1 + e2)


def _swa_prompt(proj, tables, n_p):
    sb = SWA_SUPER
    assert n_p % sb == 0
    kcol, vcol = SWA_NQ // HEAD_DIM, (SWA_NQ + SWA_NKV) // HEAD_DIM
    blk = lambda f: pl.BlockSpec((sb, HEAD_DIM), f)
    tspec = lambda g: pl.BlockSpec((None, SWA_BLK, 2 * SWA_BLK), lambda h, s: (g * SWA_HEADS + h, 0, 0))
    prev = lambda s: jnp.maximum(s - 1, 0)
    return pl.pallas_call(
        _swa_prompt_kernel,
        grid=(SWA_HEADS, n_p // sb),
        in_specs=[blk(lambda h, s: (s, h)), blk(lambda h, s: (s, SWA_HEADS + h)), blk(lambda h, s: (s, 2 * SWA_HEADS + h)),
                  blk(lambda h, s: (s, kcol + h)), blk(lambda h, s: (prev(s), kcol + h)),
                  blk(lambda h, s: (s, vcol + h)), blk(lambda h, s: (prev(s), vcol + h)),
                  tspec(0), tspec(1), tspec(2)],
        out_specs=blk(lambda h, s: (s, h)),
        out_shape=jax.ShapeDtypeStruct((n_p, SWA_NKV), F32),
        scratch_shapes=[pltpu.VMEM((2 * sb, HEAD_DIM), F32), pltpu.VMEM((2 * sb, HEAD_DIM), F32),
                        pltpu.VMEM((N_DIL, sb, HEAD_DIM), F32), pltpu.VMEM((N_DIL, sb, HEAD_DIM), F32)],
        compiler_params=_params("arbitrary", "arbitrary"),
        name="swa_prompt",
    )(proj, proj, proj, proj, proj, proj, proj, tables, tables, tables)


SWA_KC = 512
SWA_QROWS = N_DIL * SWA_HEADS


def _sample_bias_tables(biases, l_cache, t_s):
    n = np.arange(l_cache)[None, :]
    i = np.arange(t_s)[:, None]
    nn = np.arange(LANES)[None, :]
    tabs_c, tabs_n = [], []
    for b, (w, d) in zip(biases, DIL_GROUPS):
        dist = l_cache + i - n
        valid = (dist % d == 0) & (dist <= w)
        tc = jnp.where(valid[None], b[:, np.clip(dist // d, 0, w // d)], NEG_BIG)
        dist_n = i - nn
        valid_n = (dist_n >= 0) & (dist_n % d == 0) & (nn < t_s)
        tn = jnp.where(valid_n[None], b[:, np.clip(dist_n // d, 0, w // d)], NEG_BIG)
        tabs_c.append(tc.reshape(SWA_HEADS * t_s, l_cache))
        tabs_n.append(tn.reshape(SWA_HEADS * t_s, LANES))
    return jnp.concatenate(tabs_c, axis=0), jnp.concatenate(tabs_n, axis=0)


def _swa_sample_kernel(q_ref, kn_ref, vn_ref, kc_ref, vc_ref, tc_ref, tn_ref, o_ref,
                       qbd, kd, vd, m_s, l_s, acc, *, t_s, kc_sz):
    c = pl.program_id(1)
    rows_g = SWA_HEADS * t_s

    @pl.when(c == 0)
    def _():
        row_head = lax.broadcasted_iota(I32, (rows_g, SWA_NKV), 0) // t_s
        col_head = lax.broadcasted_iota(I32, (rows_g, SWA_NKV), 1) // HEAD_DIM
        own = row_head == col_head
        for g in range(N_DIL):
            qg = q_ref[:, g * SWA_NKV:(g + 1) * SWA_NKV]
            qbd[g * rows_g:(g + 1) * rows_g, :] = jnp.where(own, jnp.tile(qg, (SWA_HEADS, 1)), 0.0).astype(BF16)
        m_s[...] = jnp.full(m_s.shape, NEG_BIG, F32)
        l_s[...] = jnp.zeros(l_s.shape, F32)
        acc[...] = jnp.zeros(acc.shape, F32)

    def absorb(kmat, vmat, tab):
        s = _dot_nt(qbd[...], kmat) * SWA_SCALE + tab
        m_old = m_s[...]
        m_new = jnp.maximum(m_old, jnp.max(s, axis=-1, keepdims=True))
        alpha = jnp.exp(m_old - m_new)
        p = jnp.exp(s - m_new)
        l_s[...] = alpha * l_s[...] + jnp.sum(p, axis=-1, keepdims=True)
        acc[...] = alpha * acc[...] + _dot(p.astype(BF16), vmat)
        m_s[...] = m_new

    for h in range(SWA_HEADS):
        kd[:, h * HEAD_DIM:(h + 1) * HEAD_DIM] = kc_ref[pl.ds(h, kc_sz, stride=SWA_HEADS), :].astype(BF16)
        vd[:, h * HEAD_DIM:(h + 1) * HEAD_DIM] = vc_ref[pl.ds(h, kc_sz, stride=SWA_HEADS), :].astype(BF16)
    off = pl.multiple_of(c * kc_sz, kc_sz)
    absorb(kd[...], vd[...], tc_ref[:, pl.ds(off, kc_sz)])

    @pl.when(c == pl.num_programs(1) - 1)
    def _():
        pad = jnp.zeros((LANES - t_s, SWA_NKV), F32)
        absorb(jnp.concatenate([kn_ref[...], pad], axis=0).astype(BF16),
               jnp.concatenate([vn_ref[...], pad], axis=0).astype(BF16), tn_ref[...])
        l_fin = l_s[...]
        inv = 1.0 / l_fin
        lse = m_s[...] + jnp.log(l_fin)
        for h in range(SWA_HEADS):
            outs, lses = [], []
            for g in range(N_DIL):
                r0 = g * rows_g + h * t_s
                outs.append(acc[r0:r0 + t_s, h * HEAD_DIM:(h + 1) * HEAD_DIM] * inv[r0:r0 + t_s])
                lses.append(lse[r0:r0 + t_s])
            mx = jnp.maximum(jnp.maximum(lses[0], lses[1]), lses[2])
            e = [jnp.exp(x - mx) for x in lses]
            o_ref[:, h * HEAD_DIM:(h + 1) * HEAD_DIM] = (e[0] * outs[0] + e[1] * outs[1] + e[2] * outs[2]) / (e[0] + e[1] + e[2])


def _swa_sample(proj, cache_k, cache_v, tab_c, tab_n, n_p, n_b, t_s):
    l_cache = cache_k.shape[1]
    kc_sz = SWA_KC
    assert l_cache % kc_sz == 0 and n_p % t_s == 0 and t_s % SUBLANES == 0
    ck = cache_k.reshape(n_b, l_cache * SWA_HEADS, HEAD_DIM)
    cv = cache_v.reshape(n_b, l_cache * SWA_HEADS, HEAD_DIM)
    rb0 = n_p // t_s
    rows = N_DIL * SWA_HEADS * t_s
    cspec = pl.BlockSpec((None, kc_sz * SWA_HEADS, HEAD_DIM), lambda b, c: (b, c, 0))
    return pl.pallas_call(
        functools.partial(_swa_sample_kernel, t_s=t_s, kc_sz=kc_sz),
        grid=(n_b, l_cache // kc_sz),
        in_specs=[pl.BlockSpec((t_s, SWA_NQ), lambda b, c: (rb0 + b, 0)),
                  pl.BlockSpec((t_s, SWA_NKV), lambda b, c: (rb0 + b, SWA_NQ // SWA_NKV)),
                  pl.BlockSpec((t_s, SWA_NKV), lambda b, c: (rb0 + b, SWA_NQ // SWA_NKV + 1)),
                  cspec, cspec,
                  pl.BlockSpec((rows, l_cache), lambda b, c: (0, 0)),
                  pl.BlockSpec((rows, LANES), lambda b, c: (0, 0))],
        out_specs=pl.BlockSpec((t_s, SWA_NKV), lambda b, c: (b, 0)),
        out_shape=jax.ShapeDtypeStruct((n_b * t_s, SWA_NKV), F32),
        scratch_shapes=[pltpu.VMEM((rows, SWA_NKV), BF16), pltpu.VMEM((kc_sz, SWA_NKV), BF16),
                        pltpu.VMEM((kc_sz, SWA_NKV), BF16), pltpu.VMEM((rows, 1), F32), pltpu.VMEM((rows, 1), F32),
                        pltpu.VMEM((rows, SWA_NKV), F32)],
        compiler_params=_params("arbitrary", "arbitrary"),
        name="swa_sample",
    )(proj, proj, proj, ck, cv, tab_c, tab_n)


def _swa_mixer(x, n_p, n_b, t_s, cache_k, cache_v, w_in, w_out, rel_bias):
    tm = _pick_tile(x.shape[0], PROJ_TM)
    proj = _matmul(x, w_in, SWA_NQ + 2 * SWA_NKV, tm, 512)
    biases = _group_biases(rel_bias)
    o_p = _swa_prompt(proj, _prompt_bias_tables(biases), n_p)
    tab_c, tab_n = _sample_bias_tables(biases, cache_k.shape[1], t_s)
    o_s = _swa_sample(proj, cache_k, cache_v, tab_c, tab_n, n_p, n_b, t_s)
    y = _matmul(jnp.concatenate([o_p, o_s], axis=0), w_out, D_MODEL, tm, 512)
    return y, proj[:, SWA_NQ:SWA_NQ + SWA_NKV], proj[:, SWA_NQ + SWA_NKV:]


def _pad_rows(w, rows):
    return jnp.zeros((rows, w.shape[1]), w.dtype).at[:w.shape[0]].set(w)


def _pad_cols(w, cols):
    return jnp.zeros((w.shape[0], cols), w.dtype).at[:, :w.shape[1]].set(w)


def _gla_mixer(x0, n_p, n_b, t_s, state0, w_in, w_gate, b_gate, norm_w, w_out):
    tm = _pick_tile(x0.shape[0], PROJ_TM)
    proj = _matmul(x0, w_in, GLA_MAIN, tm, 512)
    glow = _matmul(x0, _pad_cols(w_in[:, GLA_MAIN:], LANES), LANES, tm, LANES)
    wg_pad = _pad_rows(w_gate, LANES)
    bg = b_gate.reshape(1, GLA_DK)
    nw = norm_w.reshape(1, GLA_DV_HEAD)
    zero_state = jnp.zeros((1, GLA_HEADS, GLA_DK_HEAD, GLA_DV_HEAD), F32)
    o_p, st_p = _gla(proj, glow, wg_pad, bg, nw, zero_state, n_seq=1, seq_len=n_p, row_start=0,
                     rows_per_step=512, chunk=GLA_CHUNK, mxu_intra=True)
    o_s, st_s = _gla(proj, glow, wg_pad, bg, nw, state0, n_seq=n_b, seq_len=t_s, row_start=n_p,
                     rows_per_step=t_s, chunk=math.gcd(t_s, GLA_CHUNK), mxu_intra=False)
    o = jnp.concatenate([o_p, o_s], axis=0)
    return _matmul(o, w_out, D_MODEL, tm, 512), st_p, st_s


def kernel(x_prompt, x_sample, state_gla, cache_swa_k, cache_swa_v, gla_w_in, gla_w_gate, gla_b_gate, gla_norm_w,
           gla_w_out, swa_w_in, swa_w_out, rel_bias, w_router, b_router, moe_w_gate, moe_w_up, moe_w_down,
           ln1_g, ln1_b, ln2_g, ln2_b):
    n_p = x_prompt.shape[0] * x_prompt.shape[1]
    n_b, t_s = x_sample.shape[0], x_sample.shape[1]
    n_s = n_b * t_s
    assert x_prompt.shape[0] == 1, "one prompt sequence"
    x = jnp.concatenate([x_prompt.reshape(n_p, D_MODEL), x_sample.reshape(n_s, D_MODEL)], axis=0)
    wr_pad = _pad_cols(w_router, LANES)
    br_pad = _pad_cols(b_router.reshape(1, N_EXPERTS), LANES)
    tm_ln = _pick_tile(n_p + n_s, LN_TM)

    gla_p, gla_s, kp, vp, ksm, vsm = [], [], [], [], [], []
    for i in range(DEPTH):
        j = i // 2
        if i % 2 == 0:
            y, st_p, st_s = _gla_mixer(x, n_p, n_b, t_s, state_gla[j], gla_w_in[j], gla_w_gate[j], gla_b_gate[j],
                                       gla_norm_w[j], gla_w_out[j])
            gla_p.append(st_p)
            gla_s.append(st_s)
        else:
            y, k_all, v_all = _swa_mixer(x, n_p, n_b, t_s, cache_swa_k[j], cache_swa_v[j], swa_w_in[j], swa_w_out[j],
                                         rel_bias)
            keep = min(SWA_MAX_WINDOW, n_p)
            kp.append(k_all[n_p - keep:n_p].reshape(1, keep, SWA_HEADS, HEAD_DIM))
            vp.append(v_all[n_p - keep:n_p].reshape(1, keep, SWA_HEADS, HEAD_DIM))
            ksm.append(k_all[n_p:].reshape(n_b, t_s, SWA_HEADS, HEAD_DIM))
            vsm.append(v_all[n_p:].reshape(n_b, t_s, SWA_HEADS, HEAD_DIM))
        x1, idx, gate = _ln_route(x, y, ln1_g[i:i + 1], ln1_b[i:i + 1], wr_pad, br_pad, tm_ln)
        last = i == DEPTH - 1
        x = _moe_ln(x1, idx, gate, moe_w_gate, moe_w_up, moe_w_down, i, ln2_g[i:i + 1], ln2_b[i:i + 1],
                    n_split=n_p if last else None)
    y_p, y_s = x
    return (y_p.reshape(x_prompt.shape), y_s.reshape(x_sample.shape), jnp.stack(gla_p), jnp.stack(gla_s),
            jnp.stack(kp), jnp.stack(vp), jnp.stack(ksm), jnp.stack(vsm))
```

```python
import functools
import math

import numpy as np
import jax
import jax.numpy as jnp
from jax import lax
from jax.experimental import pallas as pl
from jax.experimental.pallas import tpu as pltpu

F32 = jnp.float32
BF16 = jnp.bfloat16
I32 = jnp.int32

D_MODEL = 2048
DEPTH = 2
GLA_HEADS = 4
GLA_DK_HEAD = 256
GLA_DV_HEAD = 512
GLA_DK = GLA_HEADS * GLA_DK_HEAD
GLA_DV = GLA_HEADS * GLA_DV_HEAD
GLA_GATE_RANK = 16
GLA_GATE_TAU = 16.0
GLA_CHUNK = 64
GLA_SUB = 16
GLA_SEQS_PER_STEP = 4
GLA_MAIN = 2 * GLA_DK + 2 * GLA_DV
RMS_EPS = 1e-6
HEAD_DIM = 128
SWA_HEADS = 16
DIL_GROUPS = ((128, 1), (512, 4), (2048, 16))
N_DIL = len(DIL_GROUPS)
SWA_NQ = N_DIL * SWA_HEADS * HEAD_DIM
SWA_NKV = SWA_HEADS * HEAD_DIM
SWA_MAX_WINDOW = 2048
NUM_BUCKETS = 32
MAX_DISTANCE = 2048
N_EXPERTS = 16
N_EXPERT_GROUPS = 4
EXPERTS_PER_GROUP = 4
D_EXPERT = 1024
DEEPNORM_ALPHA = (2 * DEPTH) ** 0.25
LN_EPS = 1e-5

LANES = 128
SUBLANES = 8
VMEM_LIMIT_BYTES = 56 * 1024 * 1024
NEG_BIG = -1e30


PROJ_TM = 1024
LN_TM = 256


def _pick_tile(n, preferred):
    t = preferred
    while t > SUBLANES and n % t:
        t //= 2
    assert n % t == 0, (n, t)
    return t


def _params(*sem):
    return pltpu.CompilerParams(dimension_semantics=sem, vmem_limit_bytes=VMEM_LIMIT_BYTES)


def _dot(a, b):
    return jnp.dot(a, b, preferred_element_type=F32)


def _dot_nt(a, b):
    return lax.dot_general(a, b, (((1,), (1,)), ((), ())), preferred_element_type=F32)


def _dot_tn(a, b):
    return lax.dot_general(a, b, (((0,), (0,)), ((), ())), preferred_element_type=F32)


def _split3(a):
    hi = a.astype(BF16)
    r1 = a - hi.astype(F32)
    mid = r1.astype(BF16)
    lo = (r1 - mid.astype(F32)).astype(BF16)
    return hi, mid, lo


def _mm_kernel(x_ref, w_ref, o_ref, xb_ref):
    @pl.when(pl.program_id(1) == 0)
    def _():
        xb_ref[...] = x_ref[...].astype(BF16)

    o_ref[...] = _dot(xb_ref[...], w_ref[...].astype(BF16)).astype(o_ref.dtype)


def _matmul(x, w, n_out, tm, tn):
    m, k = x.shape
    assert m % tm == 0 and n_out % tn == 0 and w.shape[0] == k
    return pl.pallas_call(
        _mm_kernel,
        grid=(m // tm, n_out // tn),
        in_specs=[pl.BlockSpec((tm, k), lambda i, j: (i, 0)),
                  pl.BlockSpec((k, tn), lambda i, j: (0, j))],
        out_specs=pl.BlockSpec((tm, tn), lambda i, j: (i, j)),
        out_shape=jax.ShapeDtypeStruct((m, n_out), F32),
        scratch_shapes=[pltpu.VMEM((tm, k), BF16)],
        compiler_params=_params("arbitrary", "arbitrary"),
        name="dense_proj",
    )(x, w)


def _gla_kernel(q_ref, k_ref, v_ref, r_ref, gl_ref, wg_ref, bg_ref, nw_ref, s0_ref,
                o_ref, sf_ref, state, bsc, *, chunk, n_chunks, seqs, mxu_intra):
    c_sz = chunk
    r_sz = chunk * n_chunks
    t = pl.program_id(2)

    @pl.when(t == 0)
    def _():
        state[...] = s0_ref[...]

    rowi = lax.broadcasted_iota(I32, (c_sz, GLA_DK_HEAD), 0)
    tri = (lax.broadcasted_iota(I32, (c_sz, c_sz), 0) >= lax.broadcasted_iota(I32, (c_sz, c_sz), 1)).astype(BF16)
    ones = jnp.ones((c_sz, LANES), BF16)
    wg =wg_ref[...].astype(BF16)

    def one_chunk(sb, c):
        rows = pl.ds(pl.multiple_of(sb * r_sz + c * c_sz, c_sz), c_sz)
        q = q_ref[rows, :] * (GLA_DK_HEAD ** -0.5)
        k = k_ref[rows, :]
        v = v_ref[rows, :]
        z = _dot(gl_ref[rows, :].astype(BF16), wg) + bg_ref[...]
        la = (jnp.minimum(z, 0.0) - jnp.log(1.0 + jnp.exp(-jnp.abs(z)))) * (1.0 / GLA_GATE_TAU)
        l_hi, l_mid, l_lo = _split3(la)
        b = _dot(tri, l_hi) + (_dot(tri, l_mid) + _dot(tri, l_lo))
        dcol = _dot_tn(l_hi, ones) + (_dot_tn(l_mid, ones) + _dot_tn(l_lo, ones))
        b_last = b[c_sz - 1:c_sz, :]

        s_old = state[sb]
        o = _dot((q * jnp.exp(b)).astype(BF16), s_old.astype(BF16))

        if mxu_intra:
            n_sub = c_sz // GLA_SUB
            parts = [jnp.zeros((GLA_SUB, c_sz), F32)]
            colc = lax.broadcasted_iota(I32, (GLA_SUB, c_sz), 1)
            for blk in range(1, n_sub):
                lo = blk * GLA_SUB
                ref_b = b[lo - 1:lo, :]
                q_rel = q[lo:lo + GLA_SUB, :] * jnp.exp(b[lo:lo + GLA_SUB, :] - ref_b)
                k_rel = k * jnp.exp(jnp.minimum(ref_b - b, 0.0))
                s_blk = _dot_nt(q_rel.astype(BF16), k_rel.astype(BF16))
                parts.append(jnp.where(colc < lo, s_blk, 0.0))
            att = jnp.concatenate(parts, axis=0)

            b3 = b.reshape(n_sub, GLA_SUB, GLA_DK_HEAD)
            q3 = q.reshape(n_sub, GLA_SUB, GLA_DK_HEAD)
            k3 = k.reshape(n_sub, GLA_SUB, GLA_DK_HEAD)
            row3 = lax.broadcasted_iota(I32, b3.shape, 1)
            blk_lane0 = (lax.broadcasted_iota(I32, (c_sz, c_sz), 0) // GLA_SUB) * GLA_SUB
            lane_c = lax.broadcasted_iota(I32, (c_sz, c_sz), 1)
            for jj in range(GLA_SUB):
                e = jnp.exp(jnp.where(row3 >= jj, b3 - b3[:, jj:jj + 1, :], -jnp.inf))
                col = jnp.sum(q3 * e * k3[:, jj:jj + 1, :], axis=-1, keepdims=True).reshape(c_sz, 1)
                att = jnp.where(lane_c == blk_lane0 + jj, col, att)
            o = o + _dot(att.astype(BF16), v.astype(BF16))
        else:
            bsc[sb] = b
            qk_base = sb * r_sz + c * c_sz

            def jbody(j, acc):
                kj = k_ref[pl.ds(qk_base + j, 1), :]
                vj = v_ref[pl.ds(qk_base + j, 1), :]
                e = jnp.exp(jnp.where(rowi >= j, b - bsc[sb, pl.ds(j, 1), :], -jnp.inf))
                return acc + jnp.sum(q * e * kj, axis=-1, keepdims=True) * vj

            o = lax.fori_loop(0, c_sz, jbody, o, unroll=True)

        kt = (k * jnp.exp(b_last - b)).astype(BF16)
        decay = jnp.exp(dcol)
        ds = _dot_tn(kt, v.astype(BF16))
        state[sb] = s_old * jnp.tile(decay, (1, GLA_DV_HEAD // LANES)) + ds

        o = o * lax.rsqrt(jnp.mean(o * o, axis=-1, keepdims=True) + RMS_EPS) * nw_ref[...]
        r = r_ref[rows, :]
        o_ref[rows, :] = o * (r / (1.0 + jnp.exp(-r)))

    for sb in range(seqs):
        if n_chunks == 1:
            one_chunk(sb, 0)
        else:
            def body(c, carry, sb=sb):
                one_chunk(sb, c)
                return carry
            lax.fori_loop(0, n_chunks, body, 0)

    @pl.when(t == pl.num_programs(2) - 1)
    def _():
        sf_ref[...] = state[...]


def _gla(proj, glow, w_gate_pad, b_gate, norm_w, s0, *, n_seq, seq_len, row_start, rows_per_step, chunk,
         mxu_intra):
    r_sz = rows_per_step
    steps = seq_len // r_sz
    seqs = math.gcd(n_seq, GLA_SEQS_PER_STEP) if steps == 1 else 1
    blk_rows = seqs * r_sz
    assert seq_len % r_sz == 0 and r_sz % chunk == 0 and row_start % blk_rows == 0
    rb0 = row_start // blk_rows
    kq, kv = GLA_DK // GLA_DK_HEAD, GLA_DK // GLA_DV_HEAD

    def rowblk(b, t):
        return rb0 + b * steps + t

    st_spec = pl.BlockSpec((seqs, None, GLA_DK_HEAD, GLA_DV_HEAD), lambda b, h, t: (b, h, 0, 0))
    kern = functools.partial(_gla_kernel, chunk=chunk, n_chunks=r_sz // chunk, seqs=seqs, mxu_intra=mxu_intra)
    return pl.pallas_call(
        kern,
        grid=(n_seq // seqs, GLA_HEADS, steps),
        in_specs=[
            pl.BlockSpec((blk_rows, GLA_DK_HEAD), lambda b, h, t: (rowblk(b, t), h)),
            pl.BlockSpec((blk_rows, GLA_DK_HEAD), lambda b, h, t: (rowblk(b, t), kq + h)),
            pl.BlockSpec((blk_rows, GLA_DV_HEAD), lambda b, h, t: (rowblk(b, t), 2 * kv + h)),
            pl.BlockSpec((blk_rows, GLA_DV_HEAD), lambda b, h, t: (rowblk(b, t), 2 * kv + GLA_HEADS + h)),
            pl.BlockSpec((blk_rows, LANES), lambda b, h, t: (rowblk(b, t), 0)),
            pl.BlockSpec((LANES, GLA_DK_HEAD), lambda b, h, t: (0, h)),
            pl.BlockSpec((1, GLA_DK_HEAD), lambda b, h, t: (0, h)),
            pl.BlockSpec((1, GLA_DV_HEAD), lambda b, h, t: (0, 0)),
            st_spec,
        ],
        out_specs=[
            pl.BlockSpec((blk_rows, GLA_DV_HEAD), lambda b, h, t: (b * steps + t, h)),
            st_spec,
        ],
        out_shape=[jax.ShapeDtypeStruct((n_seq * seq_len, GLA_DV), F32),
                   jax.ShapeDtypeStruct((n_seq, GLA_HEADS, GLA_DK_HEAD, GLA_DV_HEAD), F32)],
        scratch_shapes=[pltpu.VMEM((seqs, GLA_DK_HEAD, GLA_DV_HEAD), F32),
                        pltpu.VMEM((seqs, chunk, GLA_DK_HEAD), F32)],
        compiler_params=_params("arbitrary", "arbitrary", "arbitrary"),
        name="gla_chunked",
    )(proj, proj, proj, proj, glow, w_gate_pad, b_gate, norm_w, s0)


def _layer_norm(x, g, b):
    mu = jnp.mean(x, axis=-1, keepdims=True)
    xc = x - mu
    var = jnp.mean(xc * xc, axis=-1, keepdims=True)
    return xc * lax.rsqrt(var + LN_EPS) * g + b


def _route(x, wr_ref, br_ref, idx_ref, gate_ref):
    tm = x.shape[0]
    xh, xm, xl = _split3(x)
    wh, wm, wl = _split3(wr_ref[...])
    logits = _dot(xh, wh) + ((_dot(xh, wm) + _dot(xm, wh)) + ((_dot(xm, wm) + _dot(xh, wl)) + _dot(xl, wh)))
    scores = 1.0 / (1.0 + jnp.exp(-logits))
    lane = lax.broadcasted_iota(I32, (tm, LANES), 1).astype(F32)
    sel = scores + br_ref[...]
    big = float(LANES)
    best = None
    for g in range(N_EXPERT_GROUPS):
        in_g = (lane >= g * EXPERTS_PER_GROUP) & (lane < (g + 1) * EXPERTS_PER_GROUP)
        vg = jnp.where(in_g, sel, -jnp.inf)
        m1 = jnp.max(vg, axis=-1, keepdims=True)
        i1 = jnp.min(jnp.where(vg == m1, lane, big), axis=-1, keepdims=True)
        vg2 = jnp.where(lane == i1, -jnp.inf, vg)
        m2 = jnp.max(vg2, axis=-1, keepdims=True)
        i2 = jnp.min(jnp.where(vg2 == m2, lane, big), axis=-1, keepdims=True)
        gs = m1 + m2
        if best is None:
            best, b1, b2 = gs, i1, i2
        else:
            better = gs > best
            best = jnp.where(better, gs, best)
            b1 = jnp.where(better, i1, b1)
            b2 = jnp.where(better, i2, b2)
    w1 = jnp.sum(jnp.where(lane == b1, scores, 0.0), axis=-1, keepdims=True)
    w2 = jnp.sum(jnp.where(lane == b2, scores, 0.0), axis=-1, keepdims=True)
    den = w1 + w2
    idx_ref[...] = jnp.where(lane == 0.0, b1, jnp.where(lane == 1.0, b2, 0.0)).astype(I32)
    gate_ref[...] = jnp.where(lane == 0.0, w1 / den, jnp.where(lane == 1.0, w2 / den, 0.0))


def _ln_route_kernel(x_ref, y_ref, g_ref, b_ref, wr_ref, br_ref, o_ref, idx_ref, gate_ref):
    x1 = _layer_norm(DEEPNORM_ALPHA * x_ref[...] + y_ref[...], g_ref[...], b_ref[...])
    o_ref[...] = x1
    _route(x1, wr_ref, br_ref, idx_ref, gate_ref)


def _ln_route(x, y, g, b, wr_pad, br_pad, tm):
    n, d = x.shape
    row = lambda i: (i, 0)
    fixed = lambda i: (0, 0)
    return pl.pallas_call(
        _ln_route_kernel,
        grid=(n // tm,),
        in_specs=[pl.BlockSpec((tm, d), row), pl.BlockSpec((tm, d), row),
                  pl.BlockSpec((1, d), fixed), pl.BlockSpec((1, d), fixed),
                  pl.BlockSpec((d, LANES), fixed), pl.BlockSpec((1, LANES), fixed)],
        out_specs=[pl.BlockSpec((tm, d), row), pl.BlockSpec((tm, LANES), row), pl.BlockSpec((tm, LANES), row)],
        out_shape=[jax.ShapeDtypeStruct((n, d), F32), jax.ShapeDtypeStruct((n, LANES), I32),
                   jax.ShapeDtypeStruct((n, LANES), F32)],
        compiler_params=_params("arbitrary"),
        name="deepnorm_ln_router",
    )(x, y, g, b, wr_pad, br_pad)


def _gather_kernel(src_ref, nu_ref, x_hbm, o_ref, buf, sem, *, tm):
    i = pl.program_id(0)

    @pl.when(i < nu_ref[0])
    def _():
        base = i * tm

        def row_copy(r):
            return pltpu.make_async_copy(x_hbm.at[pl.ds(src_ref[base + r], 1), :], buf.at[pl.ds(r, 1), :], sem)

        def issue(r, c):
            row_copy(r).start()
            return c

        def drain(r, c):
            row_copy(r).wait()
            return c

        lax.fori_loop(0, tm, issue, 0)
        lax.fori_loop(0, tm, drain, 0)
        o_ref[...] = buf[...].astype(BF16)

    @pl.when(i >= nu_ref[0])
    def _():
        o_ref[...] = jnp.zeros(o_ref.shape, o_ref.dtype)


def _gather_rows(x, src_row, n_used, tm):
    n_tiles = src_row.shape[0] // tm
    d = x.shape[1]
    return pl.pallas_call(
        functools.partial(_gather_kernel, tm=tm),
        grid_spec=pltpu.PrefetchScalarGridSpec(
            num_scalar_prefetch=2,
            grid=(n_tiles,),
            in_specs=[pl.BlockSpec(memory_space=pl.ANY)],
            out_specs=pl.BlockSpec((tm, d), lambda i, src, nu: (i, 0)),
            scratch_shapes=[pltpu.VMEM((tm, d), F32), pltpu.SemaphoreType.DMA(())],
        ),
        out_shape=jax.ShapeDtypeStruct((n_tiles * tm, d), BF16),
        compiler_params=_params("arbitrary"),
        name="moe_gather",
    )(src_row, n_used, x)


def _gmm1_kernel(te_ref, tb_ref, ch_ref, nu_ref, x_ref, wg_ref, wu_ref, h_ref, wgb, wub):
    i = pl.program_id(1)

    @pl.when(i < nu_ref[0])
    def _():
        @pl.when(ch_ref[i] == 1)
        def _():
            wgb[...] = wg_ref[...].astype(BF16)
            wub[...] = wu_ref[...].astype(BF16)

        x = x_ref[...]
        a = _dot(x, wgb[...])
        u = _dot(x, wub[...])
        h_ref[...] = ((a / (1.0 + jnp.exp(-a))) * u).astype(BF16)

    @pl.when(i >= nu_ref[0])
    def _():
        h_ref[...] = jnp.zeros(h_ref.shape, h_ref.dtype)


def _gmm1(xs, w_gate, w_up, layer, meta, tm, tf):
    te, tb, ch, nu = meta
    n_tiles = xs.shape[0] // tm
    d = xs.shape[1]
    f = w_gate.shape[-1]
    wspec = pl.BlockSpec((None, None, d, tf), lambda j, i, te, tb, ch, nu: (layer, te[i], 0, j))
    return pl.pallas_call(
        _gmm1_kernel,
        grid_spec=pltpu.PrefetchScalarGridSpec(
            num_scalar_prefetch=4,
            grid=(f // tf, n_tiles),
            in_specs=[pl.BlockSpec((tm, d), lambda j, i, te, tb, ch, nu: (tb[i], 0)), wspec, wspec],
            out_specs=pl.BlockSpec((tm, tf), lambda j, i, te, tb, ch, nu: (i, j)),
            scratch_shapes=[pltpu.VMEM((d, tf), BF16), pltpu.VMEM((d, tf), BF16)],
        ),
        out_shape=jax.ShapeDtypeStruct((n_tiles * tm, f), BF16),
        compiler_params=_params("arbitrary", "arbitrary"),
        name="moe_gate_up",
    )(te, tb, ch, nu, xs, w_gate, w_up)


def _gmm2_kernel(te_ref, tb_ref, ch_ref, nu_ref, h_ref, wd_ref, y_ref, wdb):
    i = pl.program_id(1)

    @pl.when(i < nu_ref[0])
    def _():
        @pl.when(ch_ref[i] == 1)
        def _():
            wdb[...] = wd_ref[...].astype(BF16)

        y_ref[...] = _dot(h_ref[...], wdb[...])

    @pl.when(i >= nu_ref[0])
    def _():
        y_ref[...] = jnp.zeros(y_ref.shape, y_ref.dtype)


def _gmm2(h, w_down, layer, meta, tm, tn):
    te, tb, ch, nu = meta
    n_tiles = h.shape[0] // tm
    f = h.shape[1]
    d = w_down.shape[-1]
    return pl.pallas_call(
        _gmm2_kernel,
        grid_spec=pltpu.PrefetchScalarGridSpec(
            num_scalar_prefetch=4,
            grid=(d // tn, n_tiles),
            in_specs=[pl.BlockSpec((tm, f), lambda j, i, te, tb, ch, nu: (tb[i], 0)),
                      pl.BlockSpec((None, None, f, tn), lambda j, i, te, tb, ch, nu: (layer, te[i], 0, j))],
            out_specs=pl.BlockSpec((tm, tn), lambda j, i, te, tb, ch, nu: (i, j)),
            scratch_shapes=[pltpu.VMEM((f, tn), BF16)],
        ),
        out_shape=jax.ShapeDtypeStruct((n_tiles * tm, d), F32),
        compiler_params=_params("arbitrary", "arbitrary"),
        name="moe_down",
    )(te, tb, ch, nu, h, w_down)


def _combine_ln_kernel(pos_ref, y_hbm, x_ref, gate_ref, g_ref, b_ref, *rest, tm, n_split_tiles):
    if n_split_tiles is None:
        (o_ref, y0, y1, sem) = rest
    else:
        (op_ref, os_ref, y0, y1, sem) = rest
    i = pl.program_id(0)
    base = i * tm

    def row_copy(r, k, buf):
        return pltpu.make_async_copy(y_hbm.at[pl.ds(pos_ref[2 * (base + r) + k], 1), :], buf.at[pl.ds(r, 1), :], sem)

    def issue(r, c):
        row_copy(r, 0, y0).start()
        row_copy(r, 1, y1).start()
        return c

    def drain(r, c):
        row_copy(r, 0, y0).wait()
        row_copy(r, 1, y1).wait()
        return c

    lax.fori_loop(0, tm, issue, 0)
    lax.fori_loop(0, tm, drain, 0)
    gate = gate_ref[...]
    moe = gate[:, 0:1] * y0[...] + gate[:, 1:2] * y1[...]
    out = _layer_norm(DEEPNORM_ALPHA * x_ref[...] + moe, g_ref[...], b_ref[...])
    if n_split_tiles is None:
        o_ref[...] = out
    else:
        @pl.when(i < n_split_tiles)
        def _():
            op_ref[...] = out

        @pl.when(i >= n_split_tiles)
        def _():
            os_ref[...] = out


def _combine_ln(y_sorted, pos, x, gate, g, b, tm, n_split=None):
    n, d = x.shape
    row = lambda i, pos: (i, 0)
    fixed = lambda i, pos: (0, 0)
    if n_split is None:
        nst = None
        out_specs = pl.BlockSpec((tm, d), row)
        out_shape = jax.ShapeDtypeStruct((n, d), F32)
    else:
        nst = n_split // tm
        out_specs = [pl.BlockSpec((tm, d), lambda i, pos: (jnp.minimum(i, nst - 1), 0)),
                     pl.BlockSpec((tm, d), lambda i, pos: (jnp.maximum(i - nst, 0), 0))]
        out_shape = [jax.ShapeDtypeStruct((n_split, d), F32), jax.ShapeDtypeStruct((n - n_split, d), F32)]
    return pl.pallas_call(
        functools.partial(_combine_ln_kernel, tm=tm, n_split_tiles=nst),
        grid_spec=pltpu.PrefetchScalarGridSpec(
            num_scalar_prefetch=1,
            grid=(n // tm,),
            in_specs=[pl.BlockSpec(memory_space=pl.ANY), pl.BlockSpec((tm, d), row),
                      pl.BlockSpec((tm, LANES), row), pl.BlockSpec((1, d), fixed), pl.BlockSpec((1, d), fixed)],
            out_specs=out_specs,
            scratch_shapes=[pltpu.VMEM((tm, d), F32), pltpu.VMEM((tm, d), F32), pltpu.SemaphoreType.DMA(())],
        ),
        out_shape=out_shape,
        compiler_params=_params("arbitrary"),
        name="moe_combine_ln",
    )(pos, y_sorted, x, gate, g, b)


def _dispatch_meta(idx, tm, n_tiles):
    n = idx.shape[0]
    e_flat = idx[:, :2].reshape(-1)
    onehot = (e_flat[:, None] == jnp.arange(N_EXPERTS, dtype=I32)[None, :]).astype(I32)
    rank = jnp.take_along_axis(jnp.cumsum(onehot, axis=0), e_flat[:, None], axis=1)[:, 0] - 1
    counts = onehot.sum(axis=0)
    ptiles = (counts + tm - 1) // tm
    tile_end = jnp.cumsum(ptiles)
    n_used = tile_end[-1]
    row_off = (tile_end - ptiles) * tm
    pos = row_off[e_flat] + rank
    tok = jnp.arange(2 * n, dtype=I32) // 2
    src_row = jnp.zeros((n_tiles * tm,), I32).at[pos].set(tok)
    tile_id = jnp.minimum(jnp.arange(n_tiles, dtype=I32), n_used - 1)
    tile_expert = jnp.searchsorted(tile_end, tile_id, side="right").astype(I32)
    changed = jnp.concatenate([jnp.ones((1,), I32), (tile_expert[1:] != tile_expert[:-1]).astype(I32)])
    nu = n_used.reshape(1).astype(I32)
    return src_row, pos.astype(I32), (tile_expert, tile_id.astype(I32), changed, nu)


MOE_TM = 512
MOE_TF = 512
MOE_TN = 2048


def _moe_ln(x1, idx, gate, w_gate, w_up, w_down, layer, g, b, n_split=None):
    n = x1.shape[0]
    n_tiles = -(-2 * n // MOE_TM) + N_EXPERTS
    src_row, pos, meta = _dispatch_meta(idx, MOE_TM, n_tiles)
    xs = _gather_rows(x1, src_row, meta[3], MOE_TM)
    h = _gmm1(xs, w_gate, w_up, layer, meta, MOE_TM, MOE_TF)
    y = _gmm2(h, w_down, layer, meta, MOE_TM, MOE_TN)
    tm = _pick_tile(n if n_split is None else math.gcd(n, n_split), LN_TM)
    return _combine_ln(y, pos, x1, gate, g, b, tm, n_split)


SWA_BLK = 128
SWA_SUPER = SWA_BLK * 16
SWA_SCALE = HEAD_DIM ** -0.5


def _t5_bucket(dist):
    max_exact = NUM_BUCKETS // 2
    d = jnp.maximum(dist.astype(F32), 1.0)
    large = max_exact + (jnp.log(d / max_exact) / math.log(MAX_DISTANCE / max_exact)
                         * (NUM_BUCKETS - max_exact)).astype(I32)
    large = jnp.minimum(large, NUM_BUCKETS - 1)
    return jnp.where(dist < max_exact, dist, large)


def _group_biases(rel_bias):
    out = []
    for g, (w, d) in enumerate(DIL_GROUPS):
        dist = d * jnp.arange(w // d + 1, dtype=I32)
        b = rel_bias[_t5_bucket(dist)]
        out.append(b[:, g * SWA_HEADS:(g + 1) * SWA_HEADS].T.astype(F32))
    return out


def _prompt_bias_tables(biases):
    period = 3 * SWA_BLK
    b_all = jnp.concatenate(biases, axis=0)
    u = jnp.concatenate([b_all[:, ::-1], jnp.full((b_all.shape[0], period - SWA_BLK - 1), NEG_BIG, F32)], axis=1)
    skew = jnp.tile(u, (1, SWA_BLK))[:, :SWA_BLK * (period - 1)].reshape(-1, SWA_BLK, period - 1)
    return skew[:, :, :2 * SWA_BLK]


def _swa_prompt_kernel(q0_ref, q1_ref, q2_ref, kc_ref, kp_ref, vc_ref, vp_ref, t0_ref, t1_ref, t2_ref,
                       o_ref, kk, vv, og, ls):
    s = pl.program_id(1)
    sb = SWA_SUPER
    kk[0:sb, :] = kp_ref[...]
    kk[sb:2 * sb, :] = kc_ref[...]
    vv[0:sb, :] = vp_ref[...]
    vv[sb:2 * sb, :] = vc_ref[...]
    col = lax.broadcasted_iota(I32, (SWA_BLK, 2 * SWA_BLK), 1)
    before_start = col < SWA_BLK

    for g, (_, d) in enumerate(DIL_GROUPS):
        q_ref = (q0_ref, q1_ref, q2_ref)[g]
        tab = (t0_ref, t1_ref, t2_ref)[g][...]
        n_mb = sb // (SWA_BLK * d)

        def block(idx, carry, q_ref=q_ref, tab=tab, d=d, n_mb=n_mb, g=g):
            r = idx // n_mb
            mb = idx % n_mb
            start = r + d * SWA_BLK * mb
            kstart = sb + start - d * SWA_BLK
            if d == 1:
                rows_q, rows_k = pl.ds(start, SWA_BLK), pl.ds(kstart, 2 * SWA_BLK)
            else:
                rows_q, rows_k = pl.ds(start, SWA_BLK, stride=d), pl.ds(kstart, 2 * SWA_BLK, stride=d)
            q = q_ref[rows_q, :].astype(BF16)
            k = kk[rows_k, :].astype(BF16)
            v = vv[rows_k, :].astype(BF16)
            lg = _dot_nt(q, k) * SWA_SCALE + tab
            no_prev = jnp.logical_and(s == 0, mb == 0)
            lg = jnp.where(jnp.logical_and(no_prev, before_start), NEG_BIG, lg)
            m = jnp.max(lg, axis=-1, keepdims=True)
            p = jnp.exp(lg - m)
            l = jnp.sum(p, axis=-1, keepdims=True)
            og[g, rows_q, :] = _dot(p.astype(BF16), v) / l
            ls[g, rows_q, :] = jnp.broadcast_to(m + jnp.log(l), (SWA_BLK, HEAD_DIM))
            return carry

        lax.fori_loop(0, sb // SWA_BLK, block, 0, unroll=4)

    l0, l1, l2 = ls[0], ls[1], ls[2]
    mx = jnp.maximum(jnp.maximum(l0, l1), l2)
    e0, e1, e2 = jnp.exp(l0 - mx), jnp.exp(l1 - mx), jnp.exp(l2 - mx)
    o_ref[...] = (e0 * og[0] + e1 * og[1] + e2 * og[2]) / (e0 + e1 + e2)


def _swa_prompt(proj, tables, n_p):
    sb = SWA_SUPER
    assert n_p % sb == 0
    kcol, vcol = SWA_NQ // HEAD_DIM, (SWA_NQ + SWA_NKV) // HEAD_DIM
    blk = lambda f: pl.BlockSpec((sb, HEAD_DIM), f)
    tspec = lambda g: pl.BlockSpec((None, SWA_BLK, 2 * SWA_BLK), lambda h, s: (g * SWA_HEADS + h, 0, 0))
    prev = lambda s: jnp.maximum(s - 1, 0)
    return pl.pallas_call(
        _swa_prompt_kernel,
        grid=(SWA_HEADS, n_p // sb),
        in_specs=[blk(lambda h, s: (s, h)), blk(lambda h, s: (s, SWA_HEADS + h)), blk(lambda h, s: (s, 2 * SWA_HEADS + h)),
                  blk(lambda h, s: (s, kcol + h)), blk(lambda h, s: (prev(s), kcol + h)),
                  blk(lambda h, s: (s, vcol + h)), blk(lambda h, s: (prev(s), vcol + h)),
                  tspec(0), tspec(1), tspec(2)],
        out_specs=blk(lambda h, s: (s, h)),
        out_shape=jax.ShapeDtypeStruct((n_p, SWA_NKV), F32),
        scratch_shapes=[pltpu.VMEM((2 * sb, HEAD_DIM), F32), pltpu.VMEM((2 * sb, HEAD_DIM), F32),
                        pltpu.VMEM((N_DIL, sb, HEAD_DIM), F32), pltpu.VMEM((N_DIL, sb, HEAD_DIM), F32)],
        compiler_params=_params("arbitrary", "arbitrary"),
        name="swa_prompt",
    )(proj, proj, proj, proj, proj, proj, proj, tables, tables, tables)


SWA_MB = 32
SWA_RES = 16


def _sample_bias_tables(biases, l_cache, t_s):
    b_near, b_mid, b_wide = biases
    n_groups = l_cache // SWA_RES
    neg = jnp.full((SWA_HEADS, 1), NEG_BIG, F32)

    def pick(b, j):
        j = np.asarray(j)
        ok = (j >= 0) & (j <= SWA_BLK)
        vals = jnp.concatenate([b, neg], axis=1)[:, np.where(ok, j, SWA_BLK + 1)]
        return jnp.moveaxis(vals, 0, -1)

    t_wide = pick(b_wide, n_groups - np.arange(n_groups))
    variant = np.arange(5)[:, None]
    t_mid = pick(b_mid, (SWA_BLK - 3 + variant) - 4 * np.arange(SWA_MB)[None, :])
    t_near = pick(b_near, SWA_BLK + 7 - np.arange(SWA_BLK + 7))
    d_new = np.arange(t_s)[:, None] - np.arange(t_s)[None, :]
    t_new = jnp.stack([pick(b, np.where((d_new >= 0) & (d_new % d == 0), d_new // d, -1))
                       for b, (_, d) in zip(biases, DIL_GROUPS)], axis=0)
    rep = lambda t: jnp.broadcast_to(t[..., None], t.shape + (HEAD_DIM,))
    return rep(t_wide), rep(t_mid), rep(t_near), rep(t_new)


def _swa_sample_kernel(q_ref, kn_ref, vn_ref, ka_ref, va_ref, kb_ref, vb_ref, tw_ref, tm_ref, tnr_ref, tnew_ref,
                       o_ref, m_s, l_s, acc, *, t_s):
    c = pl.program_id(1)
    mb = SWA_MB

    @pl.when(c == 0)
    def _():
        m_s[...] = jnp.full(m_s.shape, NEG_BIG, F32)
        l_s[...] = jnp.zeros(l_s.shape, F32)
        acc[...] = jnp.zeros(acc.shape, F32)

    def absorb(slot, q, kt, vt, bias):
        s = jnp.sum(q[None] * kt, axis=-1, keepdims=True) + bias
        m_old = m_s[slot]
        m_new = jnp.maximum(m_old, jnp.max(s, axis=0))
        alpha = jnp.exp(m_old - m_new)
        p = jnp.exp(s - m_new[None])
        l_s[slot] = alpha * l_s[slot] + jnp.sum(p, axis=0)
        acc[slot] = alpha * acc[slot] + jnp.sum(p * vt, axis=0)
        m_s[slot] = m_new

    def query(i, g):
        return q_ref[i, g] * SWA_SCALE

    def wide(i, carry):
        absorb(2 * t_s + i, query(i, 2), ka_ref[:, i], va_ref[:, i], tw_ref[pl.ds(c * mb, mb)])
        return carry

    lax.fori_loop(0, t_s, wide, 0, unroll=True)

    @pl.when(c == pl.num_programs(1) - 1)
    def _():
        def mid(i, carry):
            rho, hi = i % 4, i // 4
            kts, vts, bs = [], [], []
            for rr in range(4):
                k_src, v_src = (ka_ref, va_ref) if rr < 2 else (kb_ref, vb_ref)
                r_idx = rho + 4 * (rr % 2)
                kts.append(k_src[:, r_idx])
                vts.append(v_src[:, r_idx])
                bs.append(tm_ref[hi - rr + 3])
            absorb(t_s + i, query(i, 1), jnp.concatenate(kts, axis=0), jnp.concatenate(vts, axis=0),
                   jnp.concatenate(bs, axis=0))
            return carry

        lax.fori_loop(0, t_s, mid, 0, unroll=2)

        near_groups = SWA_BLK // SWA_RES
        n_near = near_groups * (SWA_RES // 2)

        def near(i, carry):
            kts, vts, bs = [], [], []
            for k_src, v_src, off in ((ka_ref, va_ref, 7), (kb_ref, vb_ref, 7 + SWA_RES // 2)):
                kts.append(k_src[mb - near_groups:mb].reshape(n_near, SWA_HEADS, HEAD_DIM))
                vts.append(v_src[mb - near_groups:mb].reshape(n_near, SWA_HEADS, HEAD_DIM))
                bs += [tnr_ref[pl.ds(SWA_RES * grp + off - i, SWA_RES // 2)] for grp in range(near_groups)]
            absorb(i, query(i, 0), jnp.concatenate(kts, axis=0), jnp.concatenate(vts, axis=0),
                   jnp.concatenate(bs, axis=0))
            return carry

        lax.fori_loop(0, t_s, near, 0, unroll=2)

        def fresh(i, carry):
            for g in range(N_DIL):
                absorb(g * t_s + i, query(i, g), kn_ref[...], vn_ref[...], tnew_ref[g, i])
            return carry

        lax.fori_loop(0, t_s, fresh, 0, unroll=2)

        for i in range(t_s):
            lse = [m_s[g * t_s + i] + jnp.log(l_s[g * t_s + i]) for g in range(N_DIL)]
            mx = jnp.maximum(jnp.maximum(lse[0], lse[1]), lse[2])
            e = [jnp.exp(x - mx) for x in lse]
            num = sum(e[g] * (acc[g * t_s + i] / l_s[g * t_s + i]) for g in range(N_DIL))
            o_ref[i] = num / (e[0] + e[1] + e[2])


def _swa_sample(q_new, k_new, v_new, cache_k, cache_v, tables, n_b, t_s):
    l_cache = cache_k.shape[1]
    assert t_s == SWA_RES // 2 and l_cache % (SWA_RES * SWA_MB) == 0 and l_cache >= SWA_MAX_WINDOW
    n_groups = l_cache // SWA_RES
    n_chunks = n_groups // SWA_MB
    ck = cache_k.reshape(n_b, n_groups, SWA_RES, SWA_HEADS, HEAD_DIM)
    cv = cache_v.reshape(n_b, n_groups, SWA_RES, SWA_HEADS, HEAD_DIM)
    half = (SWA_MB, SWA_RES // 2, SWA_HEADS, HEAD_DIM)
    a_spec = pl.BlockSpec((None,) + half, lambda b, c: (b, c, 0, 0, 0))
    b_spec = pl.BlockSpec((None,) + half, lambda b, c: (b, n_chunks - 1, 1, 0, 0))
    new_spec = pl.BlockSpec((None, t_s, SWA_HEADS, HEAD_DIM), lambda b, c: (b, 0, 0, 0))
    const = lambda t: pl.BlockSpec(t.shape, lambda b, c: (0,) * t.ndim)
    slots = N_DIL * t_s
    state = pltpu.VMEM((slots, SWA_HEADS, HEAD_DIM), F32)
    return pl.pallas_call(
        functools.partial(_swa_sample_kernel, t_s=t_s),
        grid=(n_b, n_chunks),
        in_specs=[pl.BlockSpec((None, t_s, N_DIL, SWA_HEADS, HEAD_DIM), lambda b, c: (b, 0, 0, 0, 0)),
                  new_spec, new_spec, a_spec, a_spec, b_spec, b_spec] + [const(t) for t in tables],
        out_specs=new_spec,
        out_shape=jax.ShapeDtypeStruct((n_b, t_s, SWA_HEADS, HEAD_DIM), F32),
        scratch_shapes=[state, state, state],
        compiler_params=_params("arbitrary", "arbitrary"),
        name="swa_sample",
    )(q_new, k_new, v_new, ck, cv, ck, cv, *tables)


def _swa_mixer(x, n_p, n_b, t_s, cache_k, cache_v, w_in, w_out, rel_bias):
    tm = _pick_tile(x.shape[0], PROJ_TM)
    proj = _matmul(x, w_in, SWA_NQ + 2 * SWA_NKV, tm, 512)
    biases = _group_biases(rel_bias)
    o_p = _swa_prompt(proj, _prompt_bias_tables(biases), n_p)
    k_all = proj[:, SWA_NQ:SWA_NQ + SWA_NKV]
    v_all = proj[:, SWA_NQ + SWA_NKV:]
    per_head = (n_b, t_s, SWA_HEADS, HEAD_DIM)
    o_s = _swa_sample(proj[n_p:, :SWA_NQ].reshape(n_b, t_s, N_DIL, SWA_HEADS, HEAD_DIM),
                      k_all[n_p:].reshape(per_head), v_all[n_p:].reshape(per_head), cache_k, cache_v,
                      _sample_bias_tables(biases, cache_k.shape[1], t_s), n_b, t_s)
    y = _matmul(jnp.concatenate([o_p, o_s.reshape(n_b * t_s, SWA_NKV)], axis=0), w_out, D_MODEL, tm, 512)
    return y, k_all, v_all


def _pad_rows(w, rows):
    return jnp.zeros((rows, w.shape[1]), w.dtype).at[:w.shape[0]].set(w)


def _pad_cols(w, cols):
    return jnp.zeros((w.shape[0], cols), w.dtype).at[:, :w.shape[1]].set(w)


def _gla_mixer(x0, n_p, n_b, t_s, state0, w_in, w_gate, b_gate, norm_w, w_out):
    tm = _pick_tile(x0.shape[0], PROJ_TM)
    proj = _matmul(x0, w_in, GLA_MAIN, tm, 512)
    glow = _matmul(x0, _pad_cols(w_in[:, GLA_MAIN:], LANES), LANES, tm, LANES)
    wg_pad = _pad_rows(w_gate, LANES)
    bg = b_gate.reshape(1, GLA_DK)
    nw = norm_w.reshape(1, GLA_DV_HEAD)
    zero_state = jnp.zeros((1, GLA_HEADS, GLA_DK_HEAD, GLA_DV_HEAD), F32)
    o_p, st_p = _gla(proj, glow, wg_pad, bg, nw, zero_state, n_seq=1, seq_len=n_p, row_start=0,
                     rows_per_step=512, chunk=GLA_CHUNK, mxu_intra=True)
    o_s, st_s = _gla(proj, glow, wg_pad, bg, nw, state0, n_seq=n_b, seq_len=t_s, row_start=n_p,
                     rows_per_step=t_s, chunk=math.gcd(t_s, GLA_CHUNK), mxu_intra=False)
    o = jnp.concatenate([o_p, o_s], axis=0)
    return _matmul(o, w_out, D_MODEL, tm, 512), st_p, st_s


def kernel(x_prompt, x_sample, state_gla, cache_swa_k, cache_swa_v, gla_w_in, gla_w_gate, gla_b_gate, gla_norm_w,
           gla_w_out, swa_w_in, swa_w_out, rel_bias, w_router, b_router, moe_w_gate, moe_w_up, moe_w_down,
           ln1_g, ln1_b, ln2_g, ln2_b):
    n_p = x_prompt.shape[0] * x_prompt.shape[1]
    n_b, t_s = x_sample.shape[0], x_sample.shape[1]
    n_s = n_b * t_s
    assert x_prompt.shape[0] == 1, "one prompt sequence"
    x = jnp.concatenate([x_prompt.reshape(n_p, D_MODEL), x_sample.reshape(n_s, D_MODEL)], axis=0)
    wr_pad = _pad_cols(w_router, LANES)
    br_pad = _pad_cols(b_router.reshape(1, N_EXPERTS), LANES)
    tm_ln = _pick_tile(n_p + n_s, LN_TM)

    gla_p, gla_s, kp, vp, ksm, vsm = [], [], [], [], [], []
    for i in range(DEPTH):
        j = i // 2
        if i % 2 == 0:
            y, st_p, st_s = _gla_mixer(x, n_p, n_b, t_s, state_gla[j], gla_w_in[j], gla_w_gate[j], gla_b_gate[j],
                                       gla_norm_w[j], gla_w_out[j])
            gla_p.append(st_p)
            gla_s.append(st_s)
        else:
            y, k_all, v_all = _swa_mixer(x, n_p, n_b, t_s, cache_swa_k[j], cache_swa_v[j], swa_w_in[j], swa_w_out[j],
                                         rel_bias)
            keep = min(SWA_MAX_WINDOW, n_p)
            kp.append(k_all[n_p - keep:n_p].reshape(1, keep, SWA_HEADS, HEAD_DIM))
            vp.append(v_all[n_p - keep:n_p].reshape(1, keep, SWA_HEADS, HEAD_DIM))
            ksm.append(k_all[n_p:].reshape(n_b, t_s, SWA_HEADS, HEAD_DIM))
            vsm.append(v_all[n_p:].reshape(n_b, t_s, SWA_HEADS, HEAD_DIM))
        x1, idx, gate = _ln_route(x, y, ln1_g[i:i + 1], ln1_b[i:i + 1], wr_pad, br_pad, tm_ln)
        last = i == DEPTH - 1
        x = _moe_ln(x1, idx, gate, moe_w_gate, moe_w_up, moe_w_down, i, ln2_g[i:i + 1], ln2_b[i:i + 1],
                    n_split=n_p if last else None)
    y_p, y_s = x
    return (y_p.reshape(x_prompt.shape), y_s.reshape(x_sample.shape), jnp.stack(gla_p), jnp.stack(gla_s),
            jnp.stack(kp), jnp.stack(vp), jnp.stack(ksm), jnp.stack(vsm))
```

```python
import functools
import math

import numpy as np
import jax
import jax.numpy as jnp
from jax import lax
from jax.experimental import pallas as pl
from jax.experimental.pallas import tpu as pltpu

F32 = jnp.float32
BF16 = jnp.bfloat16
I32 = jnp.int32

D_MODEL = 2048
DEPTH = 2
GLA_HEADS = 4
GLA_DK_HEAD = 256
GLA_DV_HEAD = 512
GLA_DK = GLA_HEADS * GLA_DK_HEAD
GLA_DV = GLA_HEADS * GLA_DV_HEAD
GLA_GATE_RANK = 16
GLA_GATE_TAU = 16.0
GLA_CHUNK = 64
GLA_SUB = 16
GLA_SEQS_PER_STEP = 4
GLA_MAIN = 2 * GLA_DK + 2 * GLA_DV
RMS_EPS = 1e-6
HEAD_DIM = 128
SWA_HEADS = 16
DIL_GROUPS = ((128, 1), (512, 4), (2048, 16))
N_DIL = len(DIL_GROUPS)
SWA_NQ = N_DIL * SWA_HEADS * HEAD_DIM
SWA_NKV = SWA_HEADS * HEAD_DIM
SWA_MAX_WINDOW = 2048
NUM_BUCKETS = 32
MAX_DISTANCE = 2048
N_EXPERTS = 16
N_EXPERT_GROUPS = 4
EXPERTS_PER_GROUP = 4
D_EXPERT = 1024
DEEPNORM_ALPHA = (2 * DEPTH) ** 0.25
LN_EPS = 1e-5

LANES = 128
SUBLANES = 8
VMEM_LIMIT_BYTES = 56 * 1024 * 1024
NEG_BIG = -1e30


PROJ_TM = 1024
LN_TM = 256


def _pick_tile(n, preferred):
    t = preferred
    while t > SUBLANES and n % t:
        t //= 2
    assert n % t == 0, (n, t)
    return t


def _params(*sem):
    return pltpu.CompilerParams(dimension_semantics=sem, vmem_limit_bytes=VMEM_LIMIT_BYTES)


def _dot(a, b):
    return jnp.dot(a, b, preferred_element_type=F32)


def _dot_nt(a, b):
    return lax.dot_general(a, b, (((1,), (1,)), ((), ())), preferred_element_type=F32)


def _dot_tn(a, b):
    return lax.dot_general(a, b, (((0,), (0,)), ((), ())), preferred_element_type=F32)


def _split3(a):
    hi = a.astype(BF16)
    r1 = a - hi.astype(F32)
    mid = r1.astype(BF16)
    lo = (r1 - mid.astype(F32)).astype(BF16)
    return hi, mid, lo


def _mm_kernel(x_ref, w_ref, o_ref, xb_ref):
    @pl.when(pl.program_id(1) == 0)
    def _():
        xb_ref[...] = x_ref[...].astype(BF16)

    o_ref[...] = _dot(xb_ref[...], w_ref[...].astype(BF16)).astype(o_ref.dtype)


def _matmul(x, w, n_out, tm, tn):
    m, k = x.shape
    assert m % tm == 0 and n_out % tn == 0 and w.shape[0] == k
    return pl.pallas_call(
        _mm_kernel,
        grid=(m // tm, n_out // tn),
        in_specs=[pl.BlockSpec((tm, k), lambda i, j: (i, 0)),
                  pl.BlockSpec((k, tn), lambda i, j: (0, j))],
        out_specs=pl.BlockSpec((tm, tn), lambda i, j: (i, j)),
        out_shape=jax.ShapeDtypeStruct((m, n_out), F32),
        scratch_shapes=[pltpu.VMEM((tm, k), BF16)],
        compiler_params=_params("arbitrary", "arbitrary"),
        name="dense_proj",
    )(x, w)


def _gla_kernel(q_ref, k_ref, v_ref, r_ref, gl_ref, wg_ref, bg_ref, nw_ref, s0_ref,
                o_ref, sf_ref, state, bsc, *, chunk, n_chunks, seqs, mxu_intra):
    c_sz = chunk
    r_sz = chunk * n_chunks
    t = pl.program_id(2)

    @pl.when(t == 0)
    def _():
        state[...] = s0_ref[...]

    rowi = lax.broadcasted_iota(I32, (c_sz, GLA_DK_HEAD), 0)
    tri = (lax.broadcasted_iota(I32, (c_sz, c_sz), 0) >= lax.broadcasted_iota(I32, (c_sz, c_sz), 1)).astype(BF16)
    ones = jnp.ones((c_sz, LANES), BF16)
    wg =wg_ref[...].astype(BF16)

    def one_chunk(sb, c):
        rows = pl.ds(pl.multiple_of(sb * r_sz + c * c_sz, c_sz), c_sz)
        q = q_ref[rows, :] * (GLA_DK_HEAD ** -0.5)
        k = k_ref[rows, :]
        v = v_ref[rows, :]
        z = _dot(gl_ref[rows, :].astype(BF16), wg) + bg_ref[...]
        la = (jnp.minimum(z, 0.0) - jnp.log(1.0 + jnp.exp(-jnp.abs(z)))) * (1.0 / GLA_GATE_TAU)
        l_hi, l_mid, l_lo = _split3(la)
        b = _dot(tri, l_hi) + (_dot(tri, l_mid) + _dot(tri, l_lo))
        dcol = _dot_tn(l_hi, ones) + (_dot_tn(l_mid, ones) + _dot_tn(l_lo, ones))
        b_last = b[c_sz - 1:c_sz, :]

        s_old = state[sb]
        o = _dot((q * jnp.exp(b)).astype(BF16), s_old.astype(BF16))

        if mxu_intra:
            n_sub = c_sz // GLA_SUB
            parts = [jnp.zeros((GLA_SUB, c_sz), F32)]
            colc = lax.broadcasted_iota(I32, (GLA_SUB, c_sz), 1)
            for blk in range(1, n_sub):
                lo = blk * GLA_SUB
                ref_b = b[lo - 1:lo, :]
                q_rel = q[lo:lo + GLA_SUB, :] * jnp.exp(b[lo:lo + GLA_SUB, :] - ref_b)
                k_rel = k * jnp.exp(jnp.minimum(ref_b - b, 0.0))
                s_blk = _dot_nt(q_rel.astype(BF16), k_rel.astype(BF16))
                parts.append(jnp.where(colc < lo, s_blk, 0.0))
            att = jnp.concatenate(parts, axis=0)

            b3 = b.reshape(n_sub, GLA_SUB, GLA_DK_HEAD)
            q3 = q.reshape(n_sub, GLA_SUB, GLA_DK_HEAD)
            k3 = k.reshape(n_sub, GLA_SUB, GLA_DK_HEAD)
            row3 = lax.broadcasted_iota(I32, b3.shape, 1)
            blk_lane0 = (lax.broadcasted_iota(I32, (c_sz, c_sz), 0) // GLA_SUB) * GLA_SUB
            lane_c = lax.broadcasted_iota(I32, (c_sz, c_sz), 1)
            for jj in range(GLA_SUB):
                e = jnp.exp(jnp.where(row3 >= jj, b3 - b3[:, jj:jj + 1, :], -jnp.inf))
                col = jnp.sum(q3 * e * k3[:, jj:jj + 1, :], axis=-1, keepdims=True).reshape(c_sz, 1)
                att = jnp.where(lane_c == blk_lane0 + jj, col, att)
            o = o + _dot(att.astype(BF16), v.astype(BF16))
        else:
            bsc[sb] = b
            qk_base = sb * r_sz + c * c_sz

            def jbody(j, acc):
                kj = k_ref[pl.ds(qk_base + j, 1), :]
                vj = v_ref[pl.ds(qk_base + j, 1), :]
                e = jnp.exp(jnp.where(rowi >= j, b - bsc[sb, pl.ds(j, 1), :], -jnp.inf))
                return acc + jnp.sum(q * e * kj, axis=-1, keepdims=True) * vj

            o = lax.fori_loop(0, c_sz, jbody, o, unroll=True)

        kt = (k * jnp.exp(b_last - b)).astype(BF16)
        decay = jnp.exp(dcol)
        ds = _dot_tn(kt, v.astype(BF16))
        state[sb] = s_old * jnp.tile(decay, (1, GLA_DV_HEAD // LANES)) + ds

        o = o * lax.rsqrt(jnp.mean(o * o, axis=-1, keepdims=True) + RMS_EPS) * nw_ref[...]
        r = r_ref[rows, :]
        o_ref[rows, :] = o * (r / (1.0 + jnp.exp(-r)))

    for sb in range(seqs):
        if n_chunks == 1:
            one_chunk(sb, 0)
        else:
            def body(c, carry, sb=sb):
                one_chunk(sb, c)
                return carry
            lax.fori_loop(0, n_chunks, body, 0, unroll=2)

    @pl.when(t == pl.num_programs(2) - 1)
    def _():
        sf_ref[...] = state[...]


def _gla(proj, glow, w_gate_pad, b_gate, norm_w, s0, *, n_seq, seq_len, row_start, rows_per_step, chunk,
         mxu_intra):
    r_sz = rows_per_step
    steps = seq_len // r_sz
    seqs = math.gcd(n_seq, GLA_SEQS_PER_STEP) if steps == 1 else 1
    blk_rows = seqs * r_sz
    assert seq_len % r_sz == 0 and r_sz % chunk == 0 and row_start % blk_rows == 0
    rb0 = row_start // blk_rows
    kq, kv = GLA_DK // GLA_DK_HEAD, GLA_DK // GLA_DV_HEAD

    def rowblk(b, t):
        return rb0 + b * steps + t

    st_spec = pl.BlockSpec((seqs, None, GLA_DK_HEAD, GLA_DV_HEAD), lambda b, h, t: (b, h, 0, 0))
    kern = functools.partial(_gla_kernel, chunk=chunk, n_chunks=r_sz // chunk, seqs=seqs, mxu_intra=mxu_intra)
    return pl.pallas_call(
        kern,
        grid=(n_seq // seqs, GLA_HEADS, steps),
        in_specs=[
            pl.BlockSpec((blk_rows, GLA_DK_HEAD), lambda b, h, t: (rowblk(b, t), h)),
            pl.BlockSpec((blk_rows, GLA_DK_HEAD), lambda b, h, t: (rowblk(b, t), kq + h)),
            pl.BlockSpec((blk_rows, GLA_DV_HEAD), lambda b, h, t: (rowblk(b, t), 2 * kv + h)),
            pl.BlockSpec((blk_rows, GLA_DV_HEAD), lambda b, h, t: (rowblk(b, t), 2 * kv + GLA_HEADS + h)),
            pl.BlockSpec((blk_rows, LANES), lambda b, h, t: (rowblk(b, t), 0)),
            pl.BlockSpec((LANES, GLA_DK_HEAD), lambda b, h, t: (0, h)),
            pl.BlockSpec((1, GLA_DK_HEAD), lambda b, h, t: (0, h)),
            pl.BlockSpec((1, GLA_DV_HEAD), lambda b, h, t: (0, 0)),
            st_spec,
        ],
        out_specs=[
            pl.BlockSpec((blk_rows, GLA_DV_HEAD), lambda b, h, t: (b * steps + t, h)),
            st_spec,
        ],
        out_shape=[jax.ShapeDtypeStruct((n_seq * seq_len, GLA_DV), F32),
                   jax.ShapeDtypeStruct((n_seq, GLA_HEADS, GLA_DK_HEAD, GLA_DV_HEAD), F32)],
        scratch_shapes=[pltpu.VMEM((seqs, GLA_DK_HEAD, GLA_DV_HEAD), F32),
                        pltpu.VMEM((seqs, chunk, GLA_DK_HEAD), F32)],
        compiler_params=_params("arbitrary", "arbitrary", "arbitrary"),
        name="gla_chunked",
    )(proj, proj, proj, proj, glow, w_gate_pad, b_gate, norm_w, s0)


def _layer_norm(x, g, b):
    mu = jnp.mean(x, axis=-1, keepdims=True)
    xc = x - mu
    var = jnp.mean(xc * xc, axis=-1, keepdims=True)
    return xc * lax.rsqrt(var + LN_EPS) * g + b


def _route(x, wr_ref, br_ref, idx_ref, gate_ref):
    tm = x.shape[0]
    xh, xm, xl = _split3(x)
    wh, wm, wl = _split3(wr_ref[...])
    logits = _dot(xh, wh) + ((_dot(xh, wm) + _dot(xm, wh)) + ((_dot(xm, wm) + _dot(xh, wl)) + _dot(xl, wh)))
    scores = 1.0 / (1.0 + jnp.exp(-logits))
    lane = lax.broadcasted_iota(I32, (tm, LANES), 1).astype(F32)
    sel = scores + br_ref[...]
    big = float(LANES)
    best = None
    for g in range(N_EXPERT_GROUPS):
        in_g = (lane >= g * EXPERTS_PER_GROUP) & (lane < (g + 1) * EXPERTS_PER_GROUP)
        vg = jnp.where(in_g, sel, -jnp.inf)
        m1 = jnp.max(vg, axis=-1, keepdims=True)
        i1 = jnp.min(jnp.where(vg == m1, lane, big), axis=-1, keepdims=True)
        vg2 = jnp.where(lane == i1, -jnp.inf, vg)
        m2 = jnp.max(vg2, axis=-1, keepdims=True)
        i2 = jnp.min(jnp.where(vg2 == m2, lane, big), axis=-1, keepdims=True)
        gs = m1 + m2
        if best is None:
            best, b1, b2 = gs, i1, i2
        else:
            better = gs > best
            best = jnp.where(better, gs, best)
            b1 = jnp.where(better, i1, b1)
            b2 = jnp.where(better, i2, b2)
    w1 = jnp.sum(jnp.where(lane == b1, scores, 0.0), axis=-1, keepdims=True)
    w2 = jnp.sum(jnp.where(lane == b2, scores, 0.0), axis=-1, keepdims=True)
    den = w1 + w2
    idx_ref[...] = jnp.where(lane == 0.0, b1, jnp.where(lane == 1.0, b2, 0.0)).astype(I32)
    gate_ref[...] = jnp.where(lane == 0.0, w1 / den, jnp.where(lane == 1.0, w2 / den, 0.0))


def _ln_route_kernel(x_ref, y_ref, g_ref, b_ref, wr_ref, br_ref, o_ref, idx_ref, gate_ref):
    x1 = _layer_norm(DEEPNORM_ALPHA * x_ref[...] + y_ref[...], g_ref[...], b_ref[...])
    o_ref[...] = x1
    _route(x1, wr_ref, br_ref, idx_ref, gate_ref)


def _ln_route(x, y, g, b, wr_pad, br_pad, tm):
    n, d = x.shape
    row = lambda i: (i, 0)
    fixed = lambda i: (0, 0)
    return pl.pallas_call(
        _ln_route_kernel,
        grid=(n // tm,),
        in_specs=[pl.BlockSpec((tm, d), row), pl.BlockSpec((tm, d), row),
                  pl.BlockSpec((1, d), fixed), pl.BlockSpec((1, d), fixed),
                  pl.BlockSpec((d, LANES), fixed), pl.BlockSpec((1, LANES), fixed)],
        out_specs=[pl.BlockSpec((tm, d), row), pl.BlockSpec((tm, LANES), row), pl.BlockSpec((tm, LANES), row)],
        out_shape=[jax.ShapeDtypeStruct((n, d), F32), jax.ShapeDtypeStruct((n, LANES), I32),
                   jax.ShapeDtypeStruct((n, LANES), F32)],
        compiler_params=_params("arbitrary"),
        name="deepnorm_ln_router",
    )(x, y, g, b, wr_pad, br_pad)


def _gmm1_kernel(src_ref, te_ref, ch_ref, nu_ref, x_hbm, wg_ref, wu_ref, h_ref, xbuf, wgb, wub, sems, *, tm):
    i = pl.program_id(0)
    n_used = nu_ref[0]

    def row_copy(tile, r, slot):
        return pltpu.make_async_copy(x_hbm.at[pl.ds(src_ref[tile * tm + r], 1), :],
                                     xbuf.at[slot, pl.ds(r, 1), :], sems.at[slot])

    def fetch(tile, slot):
        def issue(r, c):
            row_copy(tile, r, slot).start()
            return c
        lax.fori_loop(0, tm, issue, 0, unroll=DMA_UNROLL)

    @pl.when(i == 0)
    def _():
        fetch(0, 0)

    @pl.when(i + 1 < n_used)
    def _():
        fetch(i + 1, (i + 1) % 2)

    @pl.when(i < n_used)
    def _():
        slot = i % 2

        def drain(r, c):
            row_copy(i, r, slot).wait()
            return c
        lax.fori_loop(0, tm, drain, 0, unroll=DMA_UNROLL)

        @pl.when(ch_ref[i] == 1)
        def _():
            wgb[...] = wg_ref[...].astype(BF16)
            wub[...] = wu_ref[...].astype(BF16)

        x = xbuf[slot].astype(BF16)
        a = _dot(x, wgb[...])
        u = _dot(x, wub[...])
        h_ref[...] = ((a / (1.0 + jnp.exp(-a))) * u).astype(BF16)

    @pl.when(i >= n_used)
    def _():
        h_ref[...] = jnp.zeros(h_ref.shape, h_ref.dtype)


def _gmm1(x, src_row, w_gate, w_up, layer, meta, tm):
    te, _, ch, nu = meta
    n_tiles = src_row.shape[0] // tm
    d = x.shape[1]
    f = w_gate.shape[-1]
    wspec = pl.BlockSpec((None, None, d, f), lambda i, src, te, ch, nu: (layer, te[i], 0, 0))
    return pl.pallas_call(
        functools.partial(_gmm1_kernel, tm=tm),
        grid_spec=pltpu.PrefetchScalarGridSpec(
            num_scalar_prefetch=4,
            grid=(n_tiles,),
            in_specs=[pl.BlockSpec(memory_space=pl.ANY), wspec, wspec],
            out_specs=pl.BlockSpec((tm, f), lambda i, src, te, ch, nu: (i, 0)),
            scratch_shapes=[pltpu.VMEM((2, tm, d), F32), pltpu.VMEM((d, f), BF16), pltpu.VMEM((d, f), BF16),
                            pltpu.SemaphoreType.DMA((2,))],
        ),
        out_shape=jax.ShapeDtypeStruct((n_tiles * tm, f), BF16),
        compiler_params=_params("arbitrary"),
        name="moe_gather_gate_up",
    )(src_row, te, ch, nu, x, w_gate, w_up)


def _gmm2_kernel(te_ref, tb_ref, ch_ref, nu_ref, h_ref, wd_ref, y_ref, wdb):
    i = pl.program_id(1)

    @pl.when(i < nu_ref[0])
    def _():
        @pl.when(ch_ref[i] == 1)
        def _():
            wdb[...] = wd_ref[...].astype(BF16)

        y_ref[...] = _dot(h_ref[...], wdb[...])

    @pl.when(i >= nu_ref[0])
    def _():
        y_ref[...] = jnp.zeros(y_ref.shape, y_ref.dtype)


def _gmm2(h, w_down, layer, meta, tm, tn):
    te, tb, ch, nu = meta
    n_tiles = h.shape[0] // tm
    f = h.shape[1]
    d = w_down.shape[-1]
    return pl.pallas_call(
        _gmm2_kernel,
        grid_spec=pltpu.PrefetchScalarGridSpec(
            num_scalar_prefetch=4,
            grid=(d // tn, n_tiles),
            in_specs=[pl.BlockSpec((tm, f), lambda j, i, te, tb, ch, nu: (tb[i], 0)),
                      pl.BlockSpec((None, None, f, tn), lambda j, i, te, tb, ch, nu: (layer, te[i], 0, j))],
            out_specs=pl.BlockSpec((tm, tn), lambda j, i, te, tb, ch, nu: (i, j)),
            scratch_shapes=[pltpu.VMEM((f, tn), BF16)],
        ),
        out_shape=jax.ShapeDtypeStruct((n_tiles * tm, d), F32),
        compiler_params=_params("arbitrary", "arbitrary"),
        name="moe_down",
    )(te, tb, ch, nu, h, w_down)


def _combine_ln_kernel(pos_ref, y_hbm, x_ref, gate_ref, g_ref, b_ref, *rest, tm, n_split_tiles):
    if n_split_tiles is None:
        (o_ref, ybuf, sems) = rest
    else:
        (op_ref, os_ref, ybuf, sems) = rest
    i = pl.program_id(0)

    def row_copy(tile, r, k, slot):
        return pltpu.make_async_copy(y_hbm.at[pl.ds(pos_ref[2 * (tile * tm + r) + k], 1), :],
                                     ybuf.at[slot, k, pl.ds(r, 1), :], sems.at[slot])

    def fetch(tile, slot):
        def issue(r, c):
            row_copy(tile, r, 0, slot).start()
            row_copy(tile, r, 1, slot).start()
            return c
        lax.fori_loop(0, tm, issue, 0, unroll=DMA_UNROLL)

    @pl.when(i == 0)
    def _():
        fetch(0, 0)

    @pl.when(i + 1 < pl.num_programs(0))
    def _():
        fetch(i + 1, (i + 1) % 2)

    slot = i % 2

    def drain(r, c):
        row_copy(i, r, 0, slot).wait()
        row_copy(i, r, 1, slot).wait()
        return c

    lax.fori_loop(0, tm, drain, 0, unroll=DMA_UNROLL)
    gate = gate_ref[...]
    moe = gate[:, 0:1] * ybuf[slot, 0] + gate[:, 1:2] * ybuf[slot, 1]
    out = _layer_norm(DEEPNORM_ALPHA * x_ref[...] + moe, g_ref[...], b_ref[...])
    if n_split_tiles is None:
        o_ref[...] = out
    else:
        @pl.when(i < n_split_tiles)
        def _():
            op_ref[...] = out

        @pl.when(i >= n_split_tiles)
        def _():
            os_ref[...] = out


def _combine_ln(y_sorted, pos, x, gate, g, b, tm, n_split=None):
    n, d = x.shape
    row = lambda i, pos: (i, 0)
    fixed = lambda i, pos: (0, 0)
    if n_split is None:
        nst = None
        out_specs = pl.BlockSpec((tm, d), row)
        out_shape = jax.ShapeDtypeStruct((n, d), F32)
    else:
        nst = n_split // tm
        out_specs = [pl.BlockSpec((tm, d), lambda i, pos: (jnp.minimum(i, nst - 1), 0)),
                     pl.BlockSpec((tm, d), lambda i, pos: (jnp.maximum(i - nst, 0), 0))]
        out_shape = [jax.ShapeDtypeStruct((n_split, d), F32), jax.ShapeDtypeStruct((n - n_split, d), F32)]
    return pl.pallas_call(
        functools.partial(_combine_ln_kernel, tm=tm, n_split_tiles=nst),
        grid_spec=pltpu.PrefetchScalarGridSpec(
            num_scalar_prefetch=1,
            grid=(n // tm,),
            in_specs=[pl.BlockSpec(memory_space=pl.ANY), pl.BlockSpec((tm, d), row),
                      pl.BlockSpec((tm, LANES), row), pl.BlockSpec((1, d), fixed), pl.BlockSpec((1, d), fixed)],
            out_specs=out_specs,
            scratch_shapes=[pltpu.VMEM((2, 2, tm, d), F32), pltpu.SemaphoreType.DMA((2,))],
        ),
        out_shape=out_shape,
        compiler_params=_params("arbitrary"),
        name="moe_combine_ln",
    )(pos, y_sorted, x, gate, g, b)


def _dispatch_meta(idx, tm, n_tiles):
    n = idx.shape[0]
    e_flat = idx[:, :2].reshape(-1)
    onehot = (e_flat[:, None] == jnp.arange(N_EXPERTS, dtype=I32)[None, :]).astype(I32)
    rank = jnp.take_along_axis(jnp.cumsum(onehot, axis=0), e_flat[:, None], axis=1)[:, 0] - 1
    counts = onehot.sum(axis=0)
    ptiles = (counts + tm - 1) // tm
    tile_end = jnp.cumsum(ptiles)
    n_used = tile_end[-1]
    row_off = (tile_end - ptiles) * tm
    pos = row_off[e_flat] + rank
    tok = jnp.arange(2 * n, dtype=I32) // 2
    src_row = jnp.zeros((n_tiles * tm,), I32).at[pos].set(tok)
    tile_id = jnp.minimum(jnp.arange(n_tiles, dtype=I32), n_used - 1)
    tile_expert = jnp.searchsorted(tile_end, tile_id, side="right").astype(I32)
    changed = jnp.concatenate([jnp.ones((1,), I32), (tile_expert[1:] != tile_expert[:-1]).astype(I32)])
    nu = n_used.reshape(1).astype(I32)
    return src_row, pos.astype(I32), (tile_expert, tile_id.astype(I32), changed, nu)


DMA_UNROLL = 8
MOE_TM = 256
MOE_TN = 2048


def _moe_ln(x1, idx, gate, w_gate, w_up, w_down, layer, g, b, n_split=None):
    n = x1.shape[0]
    n_tiles = -(-2 * n // MOE_TM) + N_EXPERTS
    src_row, pos, meta = _dispatch_meta(idx, MOE_TM, n_tiles)
    h = _gmm1(x1, src_row, w_gate, w_up, layer, meta, MOE_TM)
    y = _gmm2(h, w_down, layer, meta, MOE_TM, MOE_TN)
    tm = _pick_tile(n if n_split is None else math.gcd(n, n_split), LN_TM)
    return _combine_ln(y, pos, x1, gate, g, b, tm, n_split)


SWA_BLK = 128
SWA_SUPER = SWA_BLK * 16
SWA_SCALE = HEAD_DIM ** -0.5


def _t5_bucket(dist):
    max_exact = NUM_BUCKETS // 2
    d = jnp.maximum(dist.astype(F32), 1.0)
    large = max_exact + (jnp.log(d / max_exact) / math.log(MAX_DISTANCE / max_exact)
                         * (NUM_BUCKETS - max_exact)).astype(I32)
    large = jnp.minimum(large, NUM_BUCKETS - 1)
    return jnp.where(dist < max_exact, dist, large)


def _group_biases(rel_bias):
    out = []
    for g, (w, d) in enumerate(DIL_GROUPS):
        dist = d * jnp.arange(w // d + 1, dtype=I32)
        b = rel_bias[_t5_bucket(dist)]
        out.append(b[:, g * SWA_HEADS:(g + 1) * SWA_HEADS].T.astype(F32))
    return out


def _prompt_bias_tables(biases):
    period = 3 * SWA_BLK
    b_all = jnp.concatenate(biases, axis=0)
    u = jnp.concatenate([b_all[:, ::-1], jnp.full((b_all.shape[0], period - SWA_BLK - 1), NEG_BIG, F32)], axis=1)
    skew = jnp.tile(u, (1, SWA_BLK))[:, :SWA_BLK * (period - 1)].reshape(-1, SWA_BLK, period - 1)
    return skew[:, :, :2 * SWA_BLK]


def _swa_prompt_kernel(q0_ref, q1_ref, q2_ref, kc_ref, kp_ref, vc_ref, vp_ref, t0_ref, t1_ref, t2_ref,
                       o_ref, kk, vv, og, ls):
    s = pl.program_id(1)
    sb = SWA_SUPER
    kk[0:sb, :] = kp_ref[...]
    kk[sb:2 * sb, :] = kc_ref[...]
    vv[0:sb, :] = vp_ref[...]
    vv[sb:2 * sb, :] = vc_ref[...]
    col = lax.broadcasted_iota(I32, (SWA_BLK, 2 * SWA_BLK), 1)
    before_start = col < SWA_BLK

    for g, (_, d) in enumerate(DIL_GROUPS):
        q_ref = (q0_ref, q1_ref, q2_ref)[g]
        tab = (t0_ref, t1_ref, t2_ref)[g][...]
        n_mb = sb // (SWA_BLK * d)

        def block(idx, carry, q_ref=q_ref, tab=tab, d=d, n_mb=n_mb, g=g):
            r = idx // n_mb
            mb = idx % n_mb
            start = r + d * SWA_BLK * mb
            kstart = sb + start - d * SWA_BLK
            if d == 1:
                rows_q, rows_k = pl.ds(start, SWA_BLK), pl.ds(kstart, 2 * SWA_BLK)
            else:
                rows_q, rows_k = pl.ds(start, SWA_BLK, stride=d), pl.ds(kstart, 2 * SWA_BLK, stride=d)
            q = q_ref[rows_q, :].astype(BF16)
            k = kk[rows_k, :].astype(BF16)
            v = vv[rows_k, :].astype(BF16)
            lg = _dot_nt(q, k) * SWA_SCALE + tab
            no_prev = jnp.logical_and(s == 0, mb == 0)
            lg = jnp.where(jnp.logical_and(no_prev, before_start), NEG_BIG, lg)
            m = jnp.max(lg, axis=-1, keepdims=True)
            p = jnp.exp(lg - m)
            l = jnp.sum(p, axis=-1, keepdims=True)
            og[g, rows_q, :] = _dot(p.astype(BF16), v) / l
            ls[g, rows_q, :] = jnp.broadcast_to(m + jnp.log(l), (SWA_BLK, HEAD_DIM))
            return carry

        lax.fori_loop(0, sb // SWA_BLK, block, 0, unroll=8)

    l0, l1, l2 = ls[0], ls[1], ls[2]
    mx = jnp.maximum(jnp.maximum(l0, l1), l2)
    e0, e1, e2 = jnp.exp(l0 - mx), jnp.exp(l1 - mx), jnp.exp(l2 - mx)
    o_ref[...] = (e0 * og[0] + e1 * og[1] + e2 * og[2]) / (e0 + e1 + e2)


def _swa_prompt(proj, tables, n_p):
    sb = SWA_SUPER
    assert n_p % sb == 0
    kcol, vcol = SWA_NQ // HEAD_DIM, (SWA_NQ + SWA_NKV) // HEAD_DIM
    blk = lambda f: pl.BlockSpec((sb, HEAD_DIM), f)
    tspec = lambda g: pl.BlockSpec((None, SWA_BLK, 2 * SWA_BLK), lambda h, s: (g * SWA_HEADS + h, 0, 0))
    prev = lambda s: jnp.maximum(s - 1, 0)
    return pl.pallas_call(
        _swa_prompt_kernel,
        grid=(SWA_HEADS, n_p // sb),
        in_specs=[blk(lambda h, s: (s, h)), blk(lambda h, s: (s, SWA_HEADS + h)), blk(lambda h, s: (s, 2 * SWA_HEADS + h)),
                  blk(lambda h, s: (s, kcol + h)), blk(lambda h, s: (prev(s), kcol + h)),
                  blk(lambda h, s: (s, vcol + h)), blk(lambda h, s: (prev(s), vcol + h)),
                  tspec(0), tspec(1), tspec(2)],
        out_specs=blk(lambda h, s: (s, h)),
        out_shape=jax.ShapeDtypeStruct((n_p, SWA_NKV), F32),
        scratch_shapes=[pltpu.VMEM((2 * sb, HEAD_DIM), F32), pltpu.VMEM((2 * sb, HEAD_DIM), F32),
                        pltpu.VMEM((N_DIL, sb, HEAD_DIM), F32), pltpu.VMEM((N_DIL, sb, HEAD_DIM), F32)],
        compiler_params=_params("arbitrary", "arbitrary"),
        name="swa_prompt",
    )(proj, proj, proj, proj, proj, proj, proj, tables, tables, tables)


SWA_MB = 32
SWA_RES = 16
SWA_KEYS_PER_PASS = 16
LOG2_E = math.log2(math.e)


def _sample_bias_tables(biases, l_cache, t_s):
    b_near, b_mid, b_wide = biases
    n_groups = l_cache // SWA_RES
    neg = jnp.full((SWA_HEADS, 1), NEG_BIG, F32)

    def pick(b, j):
        j = np.asarray(j)
        ok = (j >= 0) & (j <= SWA_BLK)
        vals = jnp.concatenate([b, neg], axis=1)[:, np.where(ok, j, SWA_BLK + 1)]
        return jnp.moveaxis(vals, 0, -1)

    t_wide = pick(b_wide, n_groups - np.arange(n_groups))
    variant = np.arange(5)[:, None]
    t_mid = pick(b_mid, (SWA_BLK - 3 + variant) - 4 * np.arange(SWA_MB)[None, :])
    t_near = pick(b_near, SWA_BLK + 7 - np.arange(SWA_BLK + 7))
    d_new = np.arange(t_s)[:, None] - np.arange(t_s)[None, :]
    t_new = jnp.stack([pick(b, np.where((d_new >= 0) & (d_new % d == 0), d_new // d, -1))
                       for b, (_, d) in zip(biases, DIL_GROUPS)], axis=0)
    rep = lambda t: jnp.broadcast_to((t * LOG2_E)[..., None], t.shape + (HEAD_DIM,))
    return rep(t_wide), rep(t_mid), rep(t_near), rep(t_new)


def _swa_sample_kernel(q_ref, kn_ref, vn_ref, ka_ref, va_ref, kb_ref, vb_ref, tw_ref, tm_ref, tnr_ref, tnew_ref,
                       o_ref, m_s, l_s, acc, *, t_s):
    c = pl.program_id(1)
    mb = SWA_MB

    @pl.when(c == 0)
    def _():
        m_s[...] = jnp.full(m_s.shape, NEG_BIG, F32)
        l_s[...] = jnp.zeros(l_s.shape, F32)
        acc[...] = jnp.zeros(acc.shape, F32)

    def absorb(slot, q, kt, vt, bias):
        m_run, l_run, a_run = m_s[slot], l_s[slot], acc[slot]
        for lo in range(0, kt.shape[0], SWA_KEYS_PER_PASS):
            hi = min(lo + SWA_KEYS_PER_PASS, kt.shape[0])
            s = jnp.sum(q[None] * kt[lo:hi], axis=-1, keepdims=True) + bias[lo:hi]
            m_new = jnp.maximum(m_run, jnp.max(s, axis=0))
            alpha = jnp.exp2(m_run - m_new)
            p = jnp.exp2(s - m_new[None])
            l_run = alpha * l_run + jnp.sum(p, axis=0)
            a_run = alpha * a_run + jnp.sum(p * vt[lo:hi], axis=0)
            m_run = m_new
        m_s[slot], l_s[slot], acc[slot] = m_run, l_run, a_run

    def query(i, g):
        return q_ref[i, g] * (SWA_SCALE * LOG2_E)

    def wide(i, carry):
        absorb(2 * t_s + i, query(i, 2), ka_ref[:, i], va_ref[:, i], tw_ref[pl.ds(c * mb, mb)])
        return carry

    lax.fori_loop(0, t_s, wide, 0, unroll=True)

    @pl.when(c == pl.num_programs(1) - 1)
    def _():
        def mid(i, carry):
            rho, hi = i % 4, i // 4
            kts, vts, bs = [], [], []
            for rr in range(4):
                k_src, v_src = (ka_ref, va_ref) if rr < 2 else (kb_ref, vb_ref)
                r_idx = rho + 4 * (rr % 2)
                kts.append(k_src[:, r_idx])
                vts.append(v_src[:, r_idx])
                bs.append(tm_ref[hi - rr + 3])
            absorb(t_s + i, query(i, 1), jnp.concatenate(kts, axis=0), jnp.concatenate(vts, axis=0),
                   jnp.concatenate(bs, axis=0))
            return carry

        lax.fori_loop(0, t_s, mid, 0, unroll=2)

        near_groups = SWA_BLK // SWA_RES
        n_near = near_groups * (SWA_RES // 2)

        def near(i, carry):
            kts, vts, bs = [], [], []
            for k_src, v_src, off in ((ka_ref, va_ref, 7), (kb_ref, vb_ref, 7 + SWA_RES // 2)):
                kts.append(k_src[mb - near_groups:mb].reshape(n_near, SWA_HEADS, HEAD_DIM))
                vts.append(v_src[mb - near_groups:mb].reshape(n_near, SWA_HEADS, HEAD_DIM))
                bs += [tnr_ref[pl.ds(SWA_RES * grp + off - i, SWA_RES // 2)] for grp in range(near_groups)]
            absorb(i, query(i, 0), jnp.concatenate(kts, axis=0), jnp.concatenate(vts, axis=0),
                   jnp.concatenate(bs, axis=0))
            return carry

        lax.fori_loop(0, t_s, near, 0, unroll=2)

        def fresh(i, carry):
            for g in range(N_DIL):
                absorb(g * t_s + i, query(i, g), kn_ref[...], vn_ref[...], tnew_ref[g, i])
            return carry

        lax.fori_loop(0, t_s, fresh, 0, unroll=2)

        for i in range(t_s):
            lse = [m_s[g * t_s + i] + jnp.log2(l_s[g * t_s + i]) for g in range(N_DIL)]
            mx = jnp.maximum(jnp.maximum(lse[0], lse[1]), lse[2])
            e = [jnp.exp2(x - mx) for x in lse]
            num = sum(e[g] * (acc[g * t_s + i] / l_s[g * t_s + i]) for g in range(N_DIL))
            o_ref[i] = num / (e[0] + e[1] + e[2])


def _swa_sample(q_new, k_new, v_new, cache_k, cache_v, tables, n_b, t_s):
    l_cache = cache_k.shape[1]
    assert t_s == SWA_RES // 2 and l_cache % (SWA_RES * SWA_MB) == 0 and l_cache >= SWA_MAX_WINDOW
    n_groups = l_cache // SWA_RES
    n_chunks = n_groups // SWA_MB
    ck = cache_k.reshape(n_b, n_groups, SWA_RES, SWA_HEADS, HEAD_DIM)
    cv = cache_v.reshape(n_b, n_groups, SWA_RES, SWA_HEADS, HEAD_DIM)
    half = (SWA_MB, SWA_RES // 2, SWA_HEADS, HEAD_DIM)
    a_spec = pl.BlockSpec((None,) + half, lambda b, c: (b, c, 0, 0, 0))
    b_spec = pl.BlockSpec((None,) + half, lambda b, c: (b, n_chunks - 1, 1, 0, 0))
    new_spec = pl.BlockSpec((None, t_s, SWA_HEADS, HEAD_DIM), lambda b, c: (b, 0, 0, 0))
    const = lambda t: pl.BlockSpec(t.shape, lambda b, c: (0,) * t.ndim)
    slots = N_DIL * t_s
    state = pltpu.VMEM((slots, SWA_HEADS, HEAD_DIM), F32)
    return pl.pallas_call(
        functools.partial(_swa_sample_kernel, t_s=t_s),
        grid=(n_b, n_chunks),
        in_specs=[pl.BlockSpec((None, t_s, N_DIL, SWA_HEADS, HEAD_DIM), lambda b, c: (b, 0, 0, 0, 0)),
                  new_spec, new_spec, a_spec, a_spec, b_spec, b_spec] + [const(t) for t in tables],
        out_specs=new_spec,
        out_shape=jax.ShapeDtypeStruct((n_b, t_s, SWA_HEADS, HEAD_DIM), F32),
        scratch_shapes=[state, state, state],
        compiler_params=_params("arbitrary", "arbitrary"),
        name="swa_sample",
    )(q_new, k_new, v_new, ck, cv, ck, cv, *tables)


def _swa_mixer(x, n_p, n_b, t_s, cache_k, cache_v, w_in, w_out, rel_bias):
    tm = _pick_tile(x.shape[0], PROJ_TM)
    proj = _matmul(x, w_in, SWA_NQ + 2 * SWA_NKV, tm, 512)
    biases = _group_biases(rel_bias)
    o_p = _swa_prompt(proj, _prompt_bias_tables(biases), n_p)
    k_all = proj[:, SWA_NQ:SWA_NQ + SWA_NKV]
    v_all = proj[:, SWA_NQ + SWA_NKV:]
    per_head = (n_b, t_s, SWA_HEADS, HEAD_DIM)
    o_s = _swa_sample(proj[n_p:, :SWA_NQ].reshape(n_b, t_s, N_DIL, SWA_HEADS, HEAD_DIM),
                      k_all[n_p:].reshape(per_head), v_all[n_p:].reshape(per_head), cache_k, cache_v,
                      _sample_bias_tables(biases, cache_k.shape[1], t_s), n_b, t_s)
    y = _matmul(jnp.concatenate([o_p, o_s.reshape(n_b * t_s, SWA_NKV)], axis=0), w_out, D_MODEL, tm, 512)
    return y, k_all, v_all


def _pad_rows(w, rows):
    return jnp.zeros((rows, w.shape[1]), w.dtype).at[:w.shape[0]].set(w)


def _pad_cols(w, cols):
    return jnp.zeros((w.shape[0], cols), w.dtype).at[:, :w.shape[1]].set(w)


def _gla_mixer(x0, n_p, n_b, t_s, state0, w_in, w_gate, b_gate, norm_w, w_out):
    tm = _pick_tile(x0.shape[0], PROJ_TM)
    proj = _matmul(x0, w_in, GLA_MAIN, tm, 512)
    glow = _matmul(x0, _pad_cols(w_in[:, GLA_MAIN:], LANES), LANES, tm, LANES)
    wg_pad = _pad_rows(w_gate, LANES)
    bg = b_gate.reshape(1, GLA_DK)
    nw = norm_w.reshape(1, GLA_DV_HEAD)
    zero_state = jnp.zeros((1, GLA_HEADS, GLA_DK_HEAD, GLA_DV_HEAD), F32)
    o_p, st_p = _gla(proj, glow, wg_pad, bg, nw, zero_state, n_seq=1, seq_len=n_p, row_start=0,
                     rows_per_step=512, chunk=GLA_CHUNK, mxu_intra=True)
    o_s, st_s = _gla(proj, glow, wg_pad, bg, nw, state0, n_seq=n_b, seq_len=t_s, row_start=n_p,
                     rows_per_step=t_s, chunk=math.gcd(t_s, GLA_CHUNK), mxu_intra=False)
    o = jnp.concatenate([o_p, o_s], axis=0)
    return _matmul(o, w_out, D_MODEL, tm, 512), st_p, st_s


def kernel(x_prompt, x_sample, state_gla, cache_swa_k, cache_swa_v, gla_w_in, gla_w_gate, gla_b_gate, gla_norm_w,
           gla_w_out, swa_w_in, swa_w_out, rel_bias, w_router, b_router, moe_w_gate, moe_w_up, moe_w_down,
           ln1_g, ln1_b, ln2_g, ln2_b):
    n_p = x_prompt.shape[0] * x_prompt.shape[1]
    n_b, t_s = x_sample.shape[0], x_sample.shape[1]
    n_s = n_b * t_s
    assert x_prompt.shape[0] == 1, "one prompt sequence"
    x = jnp.concatenate([x_prompt.reshape(n_p, D_MODEL), x_sample.reshape(n_s, D_MODEL)], axis=0)
    wr_pad = _pad_cols(w_router, LANES)
    br_pad = _pad_cols(b_router.reshape(1, N_EXPERTS), LANES)
    tm_ln = _pick_tile(n_p + n_s, LN_TM)

    gla_p, gla_s, kp, vp, ksm, vsm = [], [], [], [], [], []
    for i in range(DEPTH):
        j = i // 2
        if i % 2 == 0:
            y, st_p, st_s = _gla_mixer(x, n_p, n_b, t_s, state_gla[j], gla_w_in[j], gla_w_gate[j], gla_b_gate[j],
                                       gla_norm_w[j], gla_w_out[j])
            gla_p.append(st_p)
            gla_s.append(st_s)
        else:
            y, k_all, v_all = _swa_mixer(x, n_p, n_b, t_s, cache_swa_k[j], cache_swa_v[j], swa_w_in[j], swa_w_out[j],
                                         rel_bias)
            keep = min(SWA_MAX_WINDOW, n_p)
            kp.append(k_all[n_p - keep:n_p].reshape(1, keep, SWA_HEADS, HEAD_DIM))
            vp.append(v_all[n_p - keep:n_p].reshape(1, keep, SWA_HEADS, HEAD_DIM))
            ksm.append(k_all[n_p:].reshape(n_b, t_s, SWA_HEADS, HEAD_DIM))
            vsm.append(v_all[n_p:].reshape(n_b, t_s, SWA_HEADS, HEAD_DIM))
        x1, idx, gate = _ln_route(x, y, ln1_g[i:i + 1], ln1_b[i:i + 1], wr_pad, br_pad, tm_ln)
        last = i == DEPTH - 1
        x = _moe_ln(x1, idx, gate, moe_w_gate, moe_w_up, moe_w_down, i, ln2_g[i:i + 1], ln2_b[i:i + 1],
                    n_split=n_p if last else None)
    y_p, y_s = x
    return (y_p.reshape(x_prompt.shape), y_s.reshape(x_sample.shape), jnp.stack(gla_p), jnp.stack(gla_s),
            jnp.stack(kp), jnp.stack(vp), jnp.stack(ksm), jnp.stack(vsm))
```

```python
import functools
import math

import numpy as np
import jax
import jax.numpy as jnp
from jax import lax
from jax.experimental import pallas as pl
from jax.experimental.pallas import tpu as pltpu

F32 = jnp.float32
BF16 = jnp.bfloat16
I32 = jnp.int32

D_MODEL = 2048
DEPTH = 2
GLA_HEADS = 4
GLA_DK_HEAD = 256
GLA_DV_HEAD = 512
GLA_DK = GLA_HEADS * GLA_DK_HEAD
GLA_DV = GLA_HEADS * GLA_DV_HEAD
GLA_GATE_RANK = 16
GLA_GATE_TAU = 16.0
GLA_CHUNK = 64
GLA_SUB = 16
GLA_SEQS_PER_STEP = 4
GLA_MAIN = 2 * GLA_DK + 2 * GLA_DV
RMS_EPS = 1e-6
HEAD_DIM = 128
SWA_HEADS = 16
DIL_GROUPS = ((128, 1), (512, 4), (2048, 16))
N_DIL = len(DIL_GROUPS)
SWA_NQ = N_DIL * SWA_HEADS * HEAD_DIM
SWA_NKV = SWA_HEADS * HEAD_DIM
SWA_MAX_WINDOW = 2048
NUM_BUCKETS = 32
MAX_DISTANCE = 2048
N_EXPERTS = 16
N_EXPERT_GROUPS = 4
EXPERTS_PER_GROUP = 4
D_EXPERT = 1024
DEEPNORM_ALPHA = (2 * DEPTH) ** 0.25
LN_EPS = 1e-5

LANES = 128
SUBLANES = 8
VMEM_LIMIT_BYTES = 56 * 1024 * 1024
NEG_BIG = -1e30


PROJ_TM = 1024
LN_TM = 256


def _pick_tile(n, preferred):
    t = preferred
    while t > SUBLANES and n % t:
        t //= 2
    assert n % t == 0, (n, t)
    return t


def _params(*sem):
    return pltpu.CompilerParams(dimension_semantics=sem, vmem_limit_bytes=VMEM_LIMIT_BYTES)


def _dot(a, b):
    return jnp.dot(a, b, preferred_element_type=F32)


def _dot_nt(a, b):
    return lax.dot_general(a, b, (((1,), (1,)), ((), ())), preferred_element_type=F32)


def _dot_tn(a, b):
    return lax.dot_general(a, b, (((0,), (0,)), ((), ())), preferred_element_type=F32)


def _split3(a):
    hi = a.astype(BF16)
    r1 = a - hi.astype(F32)
    mid = r1.astype(BF16)
    lo = (r1 - mid.astype(F32)).astype(BF16)
    return hi, mid, lo


def _mm_kernel(x_ref, w_ref, o_ref, xb_ref):
    @pl.when(pl.program_id(1) == 0)
    def _():
        xb_ref[...] = x_ref[...].astype(BF16)

    o_ref[...] = _dot(xb_ref[...], w_ref[...].astype(BF16)).astype(o_ref.dtype)


def _matmul(x, w, n_out, tm, tn):
    m, k = x.shape
    assert m % tm == 0 and n_out % tn == 0 and w.shape[0] == k
    return pl.pallas_call(
        _mm_kernel,
        grid=(m // tm, n_out // tn),
        in_specs=[pl.BlockSpec((tm, k), lambda i, j: (i, 0)),
                  pl.BlockSpec((k, tn), lambda i, j: (0, j))],
        out_specs=pl.BlockSpec((tm, tn), lambda i, j: (i, j)),
        out_shape=jax.ShapeDtypeStruct((m, n_out), F32),
        scratch_shapes=[pltpu.VMEM((tm, k), BF16)],
        compiler_params=_params("arbitrary", "arbitrary"),
        name="dense_proj",
    )(x, w)


def _gla_kernel(q_ref, k_ref, v_ref, r_ref, gl_ref, wg_ref, bg_ref, nw_ref, s0_ref,
                o_ref, sf_ref, state, bsc, *, chunk, n_chunks, seqs, mxu_intra):
    c_sz = chunk
    r_sz = chunk * n_chunks
    t = pl.program_id(2)

    @pl.when(t == 0)
    def _():
        state[...] = s0_ref[...]

    rowi = lax.broadcasted_iota(I32, (c_sz, GLA_DK_HEAD), 0)
    tri = (lax.broadcasted_iota(I32, (c_sz, c_sz), 0) >= lax.broadcasted_iota(I32, (c_sz, c_sz), 1)).astype(BF16)
    ones = jnp.ones((c_sz, LANES), BF16)
    wg =wg_ref[...].astype(BF16)

    def one_chunk(sb, c):
        rows = pl.ds(pl.multiple_of(sb * r_sz + c * c_sz, c_sz), c_sz)
        q = q_ref[rows, :] * (GLA_DK_HEAD ** -0.5)
        k = k_ref[rows, :]
        v = v_ref[rows, :]
        z = _dot(gl_ref[rows, :].astype(BF16), wg) + bg_ref[...]
        la = (jnp.minimum(z, 0.0) - jnp.log(1.0 + jnp.exp(-jnp.abs(z)))) * (1.0 / GLA_GATE_TAU)
        l_hi, l_mid, l_lo = _split3(la)
        b = _dot(tri, l_hi) + (_dot(tri, l_mid) + _dot(tri, l_lo))
        dcol = _dot_tn(l_hi, ones) + (_dot_tn(l_mid, ones) + _dot_tn(l_lo, ones))
        b_last = b[c_sz - 1:c_sz, :]

        s_old = state[sb]
        o = _dot((q * jnp.exp(b)).astype(BF16), s_old.astype(BF16))

        if mxu_intra:
            n_sub = c_sz // GLA_SUB
            parts = [jnp.zeros((GLA_SUB, c_sz), F32)]
            colc = lax.broadcasted_iota(I32, (GLA_SUB, c_sz), 1)
            for blk in range(1, n_sub):
                lo = blk * GLA_SUB
                ref_b = b[lo - 1:lo, :]
                q_rel = q[lo:lo + GLA_SUB, :] * jnp.exp(b[lo:lo + GLA_SUB, :] - ref_b)
                k_rel = k * jnp.exp(jnp.minimum(ref_b - b, 0.0))
                s_blk = _dot_nt(q_rel.astype(BF16), k_rel.astype(BF16))
                parts.append(jnp.where(colc < lo, s_blk, 0.0))
            att = jnp.concatenate(parts, axis=0)

            b3 = b.reshape(n_sub, GLA_SUB, GLA_DK_HEAD)
            q3 = q.reshape(n_sub, GLA_SUB, GLA_DK_HEAD)
            k3 = k.reshape(n_sub, GLA_SUB, GLA_DK_HEAD)
            row3 = lax.broadcasted_iota(I32, b3.shape, 1)
            blk_lane0 = (lax.broadcasted_iota(I32, (c_sz, c_sz), 0) // GLA_SUB) * GLA_SUB
            lane_c = lax.broadcasted_iota(I32, (c_sz, c_sz), 1)
            for jj in range(GLA_SUB):
                e = jnp.exp(jnp.where(row3 >= jj, b3 - b3[:, jj:jj + 1, :], -jnp.inf))
                col = jnp.sum(q3 * e * k3[:, jj:jj + 1, :], axis=-1, keepdims=True).reshape(c_sz, 1)
                att = jnp.where(lane_c == blk_lane0 + jj, col, att)
            o = o + _dot(att.astype(BF16), v.astype(BF16))
        else:
            bsc[sb] = b
            qk_base = sb * r_sz + c * c_sz

            def jbody(j, acc):
                kj = k_ref[pl.ds(qk_base + j, 1), :]
                vj = v_ref[pl.ds(qk_base + j, 1), :]
                e = jnp.exp(jnp.where(rowi >= j, b - bsc[sb, pl.ds(j, 1), :], -jnp.inf))
                return acc + jnp.sum(q * e * kj, axis=-1, keepdims=True) * vj

            o = lax.fori_loop(0, c_sz, jbody, o, unroll=True)

        kt = (k * jnp.exp(b_last - b)).astype(BF16)
        decay = jnp.exp(dcol)
        ds = _dot_tn(kt, v.astype(BF16))
        state[sb] = s_old * jnp.tile(decay, (1, GLA_DV_HEAD // LANES)) + ds

        o = o * lax.rsqrt(jnp.mean(o * o, axis=-1, keepdims=True) + RMS_EPS) * nw_ref[...]
        r = r_ref[rows, :]
        o_ref[rows, :] = o * (r / (1.0 + jnp.exp(-r)))

    for sb in range(seqs):
        if n_chunks == 1:
            one_chunk(sb, 0)
        else:
            def body(c, carry, sb=sb):
                one_chunk(sb, c)
                return carry
            lax.fori_loop(0, n_chunks, body, 0, unroll=2)

    @pl.when(t == pl.num_programs(2) - 1)
    def _():
        sf_ref[...] = state[...]


def _gla(proj, glow, w_gate_pad, b_gate, norm_w, s0, *, n_seq, seq_len, row_start, rows_per_step, chunk,
         mxu_intra):
    r_sz = rows_per_step
    steps = seq_len // r_sz
    seqs = math.gcd(n_seq, GLA_SEQS_PER_STEP) if steps == 1 else 1
    blk_rows = seqs * r_sz
    assert seq_len % r_sz == 0 and r_sz % chunk == 0 and row_start % blk_rows == 0
    rb0 = row_start // blk_rows
    kq, kv = GLA_DK // GLA_DK_HEAD, GLA_DK // GLA_DV_HEAD

    def rowblk(b, t):
        return rb0 + b * steps + t

    st_spec = pl.BlockSpec((seqs, None, GLA_DK_HEAD, GLA_DV_HEAD), lambda b, h, t: (b, h, 0, 0))
    kern = functools.partial(_gla_kernel, chunk=chunk, n_chunks=r_sz // chunk, seqs=seqs, mxu_intra=mxu_intra)
    return pl.pallas_call(
        kern,
        grid=(n_seq // seqs, GLA_HEADS, steps),
        in_specs=[
            pl.BlockSpec((blk_rows, GLA_DK_HEAD), lambda b, h, t: (rowblk(b, t), h)),
            pl.BlockSpec((blk_rows, GLA_DK_HEAD), lambda b, h, t: (rowblk(b, t), kq + h)),
            pl.BlockSpec((blk_rows, GLA_DV_HEAD), lambda b, h, t: (rowblk(b, t), 2 * kv + h)),
            pl.BlockSpec((blk_rows, GLA_DV_HEAD), lambda b, h, t: (rowblk(b, t), 2 * kv + GLA_HEADS + h)),
            pl.BlockSpec((blk_rows, LANES), lambda b, h, t: (rowblk(b, t), 0)),
            pl.BlockSpec((LANES, GLA_DK_HEAD), lambda b, h, t: (0, h)),
            pl.BlockSpec((1, GLA_DK_HEAD), lambda b, h, t: (0, h)),
            pl.BlockSpec((1, GLA_DV_HEAD), lambda b, h, t: (0, 0)),
            st_spec,
        ],
        out_specs=[
            pl.BlockSpec((blk_rows, GLA_DV_HEAD), lambda b, h, t: (b * steps + t, h)),
            st_spec,
        ],
        out_shape=[jax.ShapeDtypeStruct((n_seq * seq_len, GLA_DV), F32),
                   jax.ShapeDtypeStruct((n_seq, GLA_HEADS, GLA_DK_HEAD, GLA_DV_HEAD), F32)],
        scratch_shapes=[pltpu.VMEM((seqs, GLA_DK_HEAD, GLA_DV_HEAD), F32),
                        pltpu.VMEM((seqs, chunk, GLA_DK_HEAD), F32)],
        compiler_params=_params("arbitrary", "arbitrary", "arbitrary"),
        name="gla_chunked",
    )(proj, proj, proj, proj, glow, w_gate_pad, b_gate, norm_w, s0)


def _layer_norm(x, g, b):
    mu = jnp.mean(x, axis=-1, keepdims=True)
    xc = x - mu
    var = jnp.mean(xc * xc, axis=-1, keepdims=True)
    return xc * lax.rsqrt(var + LN_EPS) * g + b


def _route(x, wr_ref, br_ref, idx_ref, gate_ref, cnt_ref, seen):
    tm = x.shape[0]
    xh, xm, xl = _split3(x)
    wh, wm, wl = _split3(wr_ref[...])
    logits = _dot(xh, wh) + ((_dot(xh, wm) + _dot(xm, wh)) + ((_dot(xm, wm) + _dot(xh, wl)) + _dot(xl, wh)))
    scores = 1.0 / (1.0 + jnp.exp(-logits))
    lane = lax.broadcasted_iota(I32, (tm, LANES), 1).astype(F32)
    sel = scores + br_ref[...]
    big = float(LANES)
    best = None
    for g in range(N_EXPERT_GROUPS):
        in_g = (lane >= g * EXPERTS_PER_GROUP) & (lane < (g + 1) * EXPERTS_PER_GROUP)
        vg = jnp.where(in_g, sel, -jnp.inf)
        m1 = jnp.max(vg, axis=-1, keepdims=True)
        i1 = jnp.min(jnp.where(vg == m1, lane, big), axis=-1, keepdims=True)
        vg2 = jnp.where(lane == i1, -jnp.inf, vg)
        m2 = jnp.max(vg2, axis=-1, keepdims=True)
        i2 = jnp.min(jnp.where(vg2 == m2, lane, big), axis=-1, keepdims=True)
        gs = m1 + m2
        if best is None:
            best, b1, b2 = gs, i1, i2
        else:
            better = gs > best
            best = jnp.where(better, gs, best)
            b1 = jnp.where(better, i1, b1)
            b2 = jnp.where(better, i2, b2)
    w1 = jnp.sum(jnp.where(lane == b1, scores, 0.0), axis=-1, keepdims=True)
    w2 = jnp.sum(jnp.where(lane == b2, scores, 0.0), axis=-1, keepdims=True)
    den = w1 + w2
    picked = jnp.logical_or(lane == b1, lane == b2).astype(F32)
    earlier = (lax.broadcasted_iota(I32, (tm, tm), 0) > lax.broadcasted_iota(I32, (tm, tm), 1)).astype(BF16)
    before = _dot(earlier, picked.astype(BF16)) + seen[...]
    r1 = jnp.sum(jnp.where(lane == b1, before, 0.0), axis=-1, keepdims=True)
    r2 = jnp.sum(jnp.where(lane == b2, before, 0.0), axis=-1, keepdims=True)
    seen[...] = seen[...] + jnp.sum(picked, axis=0, keepdims=True)
    cnt_ref[...] = jnp.broadcast_to(seen[...], cnt_ref.shape)
    idx_ref[...] = jnp.where(lane == 0.0, b1, jnp.where(lane == 1.0, b2, jnp.where(lane == 2.0, r1, jnp.where(
        lane == 3.0, r2, 0.0)))).astype(I32)
    gate_ref[...] = jnp.where(lane == 0.0, w1 / den, jnp.where(lane == 1.0, w2 / den, 0.0))


def _ln_route_kernel(x_ref, y_ref, g_ref, b_ref, wr_ref, br_ref, o_ref, idx_ref, gate_ref, cnt_ref, seen):
    @pl.when(pl.program_id(0) == 0)
    def _():
        seen[...] = jnp.zeros(seen.shape, F32)

    x1 = _layer_norm(DEEPNORM_ALPHA * x_ref[...] + y_ref[...], g_ref[...], b_ref[...])
    o_ref[...] = x1
    _route(x1, wr_ref, br_ref, idx_ref, gate_ref, cnt_ref, seen)


def _ln_route(x, y, g, b, wr_pad, br_pad, tm):
    n, d = x.shape
    row = lambda i: (i, 0)
    fixed = lambda i: (0, 0)
    return pl.pallas_call(
        _ln_route_kernel,
        grid=(n // tm,),
        in_specs=[pl.BlockSpec((tm, d), row), pl.BlockSpec((tm, d), row),
                  pl.BlockSpec((1, d), fixed), pl.BlockSpec((1, d), fixed),
                  pl.BlockSpec((d, LANES), fixed), pl.BlockSpec((1, LANES), fixed)],
        out_specs=[pl.BlockSpec((tm, d), row), pl.BlockSpec((tm, LANES), row), pl.BlockSpec((tm, LANES), row),
                   pl.BlockSpec((SUBLANES, LANES), fixed)],
        out_shape=[jax.ShapeDtypeStruct((n, d), F32), jax.ShapeDtypeStruct((n, LANES), I32),
                   jax.ShapeDtypeStruct((n, LANES), F32), jax.ShapeDtypeStruct((SUBLANES, LANES), F32)],
        scratch_shapes=[pltpu.VMEM((1, LANES), F32)],
        compiler_params=_params("arbitrary"),
        name="deepnorm_ln_router",
    )(x, y, g, b, wr_pad, br_pad)


def _gmm1_kernel(src_ref, te_ref, ch_ref, nu_ref, x_hbm, wg_ref, wu_ref, h_ref, xbuf, wgb, wub, sems, *, tm):
    i = pl.program_id(0)
    n_used = nu_ref[0]

    def row_copy(tile, r, slot):
        return pltpu.make_async_copy(x_hbm.at[pl.ds(src_ref[tile * tm + r], 1), :],
                                     xbuf.at[slot, pl.ds(r, 1), :], sems.at[slot])

    def fetch(tile, slot):
        def issue(r, c):
            row_copy(tile, r, slot).start()
            return c
        lax.fori_loop(0, tm, issue, 0, unroll=DMA_UNROLL)

    @pl.when(i == 0)
    def _():
        fetch(0, 0)

    @pl.when(i + 1 < n_used)
    def _():
        fetch(i + 1, (i + 1) % 2)

    @pl.when(i < n_used)
    def _():
        slot = i % 2

        def drain(r, c):
            row_copy(i, r, slot).wait()
            return c
        lax.fori_loop(0, tm, drain, 0, unroll=DMA_UNROLL)

        @pl.when(ch_ref[i] == 1)
        def _():
            wgb[...] = wg_ref[...].astype(BF16)
            wub[...] = wu_ref[...].astype(BF16)

        x = xbuf[slot].astype(BF16)
        a = _dot(x, wgb[...])
        u = _dot(x, wub[...])
        h_ref[...] = ((a / (1.0 + jnp.exp(-a))) * u).astype(BF16)

    @pl.when(i >= n_used)
    def _():
        h_ref[...] = jnp.zeros(h_ref.shape, h_ref.dtype)


def _gmm1(x, src_row, w_gate, w_up, layer, meta, tm):
    te, _, ch, nu = meta
    n_tiles = src_row.shape[0] // tm
    d = x.shape[1]
    f = w_gate.shape[-1]
    wspec = pl.BlockSpec((None, None, d, f), lambda i, src, te, ch, nu: (layer, te[i], 0, 0))
    return pl.pallas_call(
        functools.partial(_gmm1_kernel, tm=tm),
        grid_spec=pltpu.PrefetchScalarGridSpec(
            num_scalar_prefetch=4,
            grid=(n_tiles,),
            in_specs=[pl.BlockSpec(memory_space=pl.ANY), wspec, wspec],
            out_specs=pl.BlockSpec((tm, f), lambda i, src, te, ch, nu: (i, 0)),
            scratch_shapes=[pltpu.VMEM((2, tm, d), F32), pltpu.VMEM((d, f), BF16), pltpu.VMEM((d, f), BF16),
                            pltpu.SemaphoreType.DMA((2,))],
        ),
        out_shape=jax.ShapeDtypeStruct((n_tiles * tm, f), BF16),
        compiler_params=_params("arbitrary"),
        name="moe_gather_gate_up",
    )(src_row, te, ch, nu, x, w_gate, w_up)


def _gmm2_kernel(te_ref, tb_ref, ch_ref, nu_ref, h_ref, wd_ref, y_ref, wdb):
    i = pl.program_id(1)

    @pl.when(i < nu_ref[0])
    def _():
        @pl.when(ch_ref[i] == 1)
        def _():
            wdb[...] = wd_ref[...].astype(BF16)

        y_ref[...] = _dot(h_ref[...], wdb[...])

    @pl.when(i >= nu_ref[0])
    def _():
        y_ref[...] = jnp.zeros(y_ref.shape, y_ref.dtype)


def _gmm2(h, w_down, layer, meta, tm, tn):
    te, tb, ch, nu = meta
    n_tiles = h.shape[0] // tm
    f = h.shape[1]
    d = w_down.shape[-1]
    return pl.pallas_call(
        _gmm2_kernel,
        grid_spec=pltpu.PrefetchScalarGridSpec(
            num_scalar_prefetch=4,
            grid=(d // tn, n_tiles),
            in_specs=[pl.BlockSpec((tm, f), lambda j, i, te, tb, ch, nu: (tb[i], 0)),
                      pl.BlockSpec((None, None, f, tn), lambda j, i, te, tb, ch, nu: (layer, te[i], 0, j))],
            out_specs=pl.BlockSpec((tm, tn), lambda j, i, te, tb, ch, nu: (i, j)),
            scratch_shapes=[pltpu.VMEM((f, tn), BF16)],
        ),
        out_shape=jax.ShapeDtypeStruct((n_tiles * tm, d), F32),
        compiler_params=_params("arbitrary", "arbitrary"),
        name="moe_down",
    )(te, tb, ch, nu, h, w_down)


def _combine_ln_kernel(pos_ref, y_hbm, x_ref, gate_ref, g_ref, b_ref, *rest, tm, n_split_tiles):
    if n_split_tiles is None:
        (o_ref, ybuf, sems) = rest
    else:
        (op_ref, os_ref, ybuf, sems) = rest
    i = pl.program_id(0)

    def row_copy(tile, r, k, slot):
        return pltpu.make_async_copy(y_hbm.at[pl.ds(pos_ref[2 * (tile * tm + r) + k], 1), :],
                                     ybuf.at[slot, k, pl.ds(r, 1), :], sems.at[slot])

    def fetch(tile, slot):
        def issue(r, c):
            row_copy(tile, r, 0, slot).start()
            row_copy(tile, r, 1, slot).start()
            return c
        lax.fori_loop(0, tm, issue, 0, unroll=DMA_UNROLL)

    @pl.when(i == 0)
    def _():
        fetch(0, 0)

    @pl.when(i + 1 < pl.num_programs(0))
    def _():
        fetch(i + 1, (i + 1) % 2)

    slot = i % 2

    def drain(r, c):
        row_copy(i, r, 0, slot).wait()
        row_copy(i, r, 1, slot).wait()
        return c

    lax.fori_loop(0, tm, drain, 0, unroll=DMA_UNROLL)
    gate = gate_ref[...]
    moe = gate[:, 0:1] * ybuf[slot, 0] + gate[:, 1:2] * ybuf[slot, 1]
    out = _layer_norm(DEEPNORM_ALPHA * x_ref[...] + moe, g_ref[...], b_ref[...])
    if n_split_tiles is None:
        o_ref[...] = out
    else:
        @pl.when(i < n_split_tiles)
        def _():
            op_ref[...] = out

        @pl.when(i >= n_split_tiles)
        def _():
            os_ref[...] = out


def _combine_ln(y_sorted, pos, x, gate, g, b, tm, n_split=None):
    n, d = x.shape
    row = lambda i, pos: (i, 0)
    fixed = lambda i, pos: (0, 0)
    if n_split is None:
        nst = None
        out_specs = pl.BlockSpec((tm, d), row)
        out_shape = jax.ShapeDtypeStruct((n, d), F32)
    else:
        nst = n_split // tm
        out_specs = [pl.BlockSpec((tm, d), lambda i, pos: (jnp.minimum(i, nst - 1), 0)),
                     pl.BlockSpec((tm, d), lambda i, pos: (jnp.maximum(i - nst, 0), 0))]
        out_shape = [jax.ShapeDtypeStruct((n_split, d), F32), jax.ShapeDtypeStruct((n - n_split, d), F32)]
    return pl.pallas_call(
        functools.partial(_combine_ln_kernel, tm=tm, n_split_tiles=nst),
        grid_spec=pltpu.PrefetchScalarGridSpec(
            num_scalar_prefetch=1,
            grid=(n // tm,),
            in_specs=[pl.BlockSpec(memory_space=pl.ANY), pl.BlockSpec((tm, d), row),
                      pl.BlockSpec((tm, LANES), row), pl.BlockSpec((1, d), fixed), pl.BlockSpec((1, d), fixed)],
            out_specs=out_specs,
            scratch_shapes=[pltpu.VMEM((2, 2, tm, d), F32), pltpu.SemaphoreType.DMA((2,))],
        ),
        out_shape=out_shape,
        compiler_params=_params("arbitrary"),
        name="moe_combine_ln",
    )(pos, y_sorted, x, gate, g, b)


def _dispatch_meta(idx, counts, tm, n_tiles):
    n = idx.shape[0]
    e_flat = idx[:, :2].reshape(-1)
    rank = idx[:, 2:4].reshape(-1)
    counts = counts[0, :N_EXPERTS].astype(I32)
    ptiles = (counts + tm - 1) // tm
    tile_end = jnp.cumsum(ptiles)
    n_used = tile_end[-1]
    row_off = (tile_end - ptiles) * tm
    pos = row_off[e_flat] + rank
    tok = jnp.arange(2 * n, dtype=I32) // 2
    src_row = jnp.zeros((n_tiles * tm,), I32).at[pos].set(tok)
    tile_id = jnp.minimum(jnp.arange(n_tiles, dtype=I32), n_used - 1)
    tile_expert = jnp.searchsorted(tile_end, tile_id, side="right").astype(I32)
    changed = jnp.concatenate([jnp.ones((1,), I32), (tile_expert[1:] != tile_expert[:-1]).astype(I32)])
    nu = n_used.reshape(1).astype(I32)
    return src_row, pos.astype(I32), (tile_expert, tile_id.astype(I32), changed, nu)


DMA_UNROLL = 8
MOE_TM = 256
MOE_TN = 2048


def _moe_ln(x1, idx, gate, counts, w_gate, w_up, w_down, layer, g, b, n_split=None):
    n = x1.shape[0]
    n_tiles = -(-2 * n // MOE_TM) + N_EXPERTS
    src_row, pos, meta = _dispatch_meta(idx, counts, MOE_TM, n_tiles)
    h = _gmm1(x1, src_row, w_gate, w_up, layer, meta, MOE_TM)
    y = _gmm2(h, w_down, layer, meta, MOE_TM, MOE_TN)
    tm = _pick_tile(n if n_split is None else math.gcd(n, n_split), LN_TM)
    return _combine_ln(y, pos, x1, gate, g, b, tm, n_split)


SWA_BLK = 128
SWA_SUPER = SWA_BLK * 16
SWA_SCALE = HEAD_DIM ** -0.5
SWA_TOEPLITZ_PERIOD = 4 * SWA_BLK


def _t5_bucket(dist):
    max_exact = NUM_BUCKETS // 2
    d = jnp.maximum(dist.astype(F32), 1.0)
    large = max_exact + (jnp.log(d / max_exact) / math.log(MAX_DISTANCE / max_exact)
                         * (NUM_BUCKETS - max_exact)).astype(I32)
    large = jnp.minimum(large, NUM_BUCKETS - 1)
    return jnp.where(dist < max_exact, dist, large)


def _group_biases(rel_bias):
    out = []
    for g, (w, d) in enumerate(DIL_GROUPS):
        dist = d * jnp.arange(w // d + 1, dtype=I32)
        b = rel_bias[_t5_bucket(dist)]
        out.append(b[:, g * SWA_HEADS:(g + 1) * SWA_HEADS].T.astype(F32))
    return out


def _prompt_bias_tables(biases):
    b_all = jnp.concatenate(biases, axis=0)
    mask = jnp.full((b_all.shape[0], SWA_TOEPLITZ_PERIOD - SWA_BLK - 1), NEG_BIG, F32)
    return jnp.concatenate([b_all[:, ::-1], mask], axis=1)[:, None, :]


def _swa_prompt_kernel(q0_ref, q1_ref, q2_ref, kc_ref, kp_ref, vc_ref, vp_ref, t0_ref, t1_ref, t2_ref,
                       o_ref, kk, vv, og, ls):
    s = pl.program_id(1)
    sb = SWA_SUPER
    kk[0:sb, :] = kp_ref[...]
    kk[sb:2 * sb, :] = kc_ref[...]
    vv[0:sb, :] = vp_ref[...]
    vv[sb:2 * sb, :] = vc_ref[...]
    col = lax.broadcasted_iota(I32, (SWA_BLK, 2 * SWA_BLK), 1)
    before_start = col < SWA_BLK

    for g, (_, d) in enumerate(DIL_GROUPS):
        q_ref = (q0_ref, q1_ref, q2_ref)[g]
        u_row = jnp.broadcast_to((t0_ref, t1_ref, t2_ref)[g][...], (SWA_BLK, SWA_TOEPLITZ_PERIOD))
        tab = pltpu.roll(u_row, 0, 1, stride=1, stride_axis=0)[:, :2 * SWA_BLK]
        n_mb = sb // (SWA_BLK * d)

        def block(idx, carry, q_ref=q_ref, tab=tab, d=d, n_mb=n_mb, g=g):
            r = idx // n_mb
            mb = idx % n_mb
            start = r + d * SWA_BLK * mb
            kstart = sb + start - d * SWA_BLK
            if d == 1:
                rows_q, rows_k = pl.ds(start, SWA_BLK), pl.ds(kstart, 2 * SWA_BLK)
            else:
                rows_q, rows_k = pl.ds(start, SWA_BLK, stride=d), pl.ds(kstart, 2 * SWA_BLK, stride=d)
            q = q_ref[rows_q, :].astype(BF16)
            k = kk[rows_k, :].astype(BF16)
            v = vv[rows_k, :].astype(BF16)
            lg = _dot_nt(q, k) * SWA_SCALE + tab
            no_prev = jnp.logical_and(s == 0, mb == 0)
            lg = jnp.where(jnp.logical_and(no_prev, before_start), NEG_BIG, lg)
            m = jnp.max(lg, axis=-1, keepdims=True)
            p = jnp.exp(lg - m)
            l = jnp.sum(p, axis=-1, keepdims=True)
            og[g, rows_q, :] = _dot(p.astype(BF16), v) / l
            ls[g, rows_q, :] = jnp.broadcast_to(m + jnp.log(l), (SWA_BLK, HEAD_DIM))
            return carry

        lax.fori_loop(0, sb // SWA_BLK, block, 0, unroll=8)

    l0, l1, l2 = ls[0], ls[1], ls[2]
    mx = jnp.maximum(jnp.maximum(l0, l1), l2)
    e0, e1, e2 = jnp.exp(l0 - mx), jnp.exp(l1 - mx), jnp.exp(l2 - mx)
    o_ref[...] = (e0 * og[0] + e1 * og[1] + e2 * og[2]) / (e0 + e1 + e2)


def _swa_prompt(proj, tables, n_p):
    sb = SWA_SUPER
    assert n_p % sb == 0
    kcol, vcol = SWA_NQ // HEAD_DIM, (SWA_NQ + SWA_NKV) // HEAD_DIM
    blk = lambda f: pl.BlockSpec((sb, HEAD_DIM), f)
    tspec = lambda g: pl.BlockSpec((None, 1, SWA_TOEPLITZ_PERIOD), lambda h, s: (g * SWA_HEADS + h, 0, 0))
    prev = lambda s: jnp.maximum(s - 1, 0)
    return pl.pallas_call(
        _swa_prompt_kernel,
        grid=(SWA_HEADS, n_p // sb),
        in_specs=[blk(lambda h, s: (s, h)), blk(lambda h, s: (s, SWA_HEADS + h)), blk(lambda h, s: (s, 2 * SWA_HEADS + h)),
                  blk(lambda h, s: (s, kcol + h)), blk(lambda h, s: (prev(s), kcol + h)),
                  blk(lambda h, s: (s, vcol + h)), blk(lambda h, s: (prev(s), vcol + h)),
                  tspec(0), tspec(1), tspec(2)],
        out_specs=blk(lambda h, s: (s, h)),
        out_shape=jax.ShapeDtypeStruct((n_p, SWA_NKV), F32),
        scratch_shapes=[pltpu.VMEM((2 * sb, HEAD_DIM), F32), pltpu.VMEM((2 * sb, HEAD_DIM), F32),
                        pltpu.VMEM((N_DIL, sb, HEAD_DIM), F32), pltpu.VMEM((N_DIL, sb, HEAD_DIM), F32)],
        compiler_params=_params("arbitrary", "arbitrary"),
        name="swa_prompt",
    )(proj, proj, proj, proj, proj, proj, proj, tables, tables, tables)


SWA_MB = 32
SWA_RES = 16
SWA_KEYS_PER_PASS = 16
LOG2_E = math.log2(math.e)


def _sample_bias_tables(biases, l_cache, t_s):
    b_near, b_mid, b_wide = biases
    n_groups = l_cache // SWA_RES
    neg = jnp.full((SWA_HEADS, 1), NEG_BIG, F32)

    def pick(b, j):
        j = np.asarray(j)
        ok = (j >= 0) & (j <= SWA_BLK)
        vals = jnp.concatenate([b, neg], axis=1)[:, np.where(ok, j, SWA_BLK + 1)]
        return jnp.moveaxis(vals, 0, -1)

    t_wide = pick(b_wide, n_groups - np.arange(n_groups))
    variant = np.arange(5)[:, None]
    t_mid = pick(b_mid, (SWA_BLK - 3 + variant) - 4 * np.arange(SWA_MB)[None, :])
    t_near = pick(b_near, SWA_BLK + 7 - np.arange(SWA_BLK + 7))
    d_new = np.arange(t_s)[:, None] - np.arange(t_s)[None, :]
    t_new = jnp.stack([pick(b, np.where((d_new >= 0) & (d_new % d == 0), d_new // d, -1))
                       for b, (_, d) in zip(biases, DIL_GROUPS)], axis=0)
    rep = lambda t: jnp.broadcast_to((t * LOG2_E)[..., None], t.shape + (HEAD_DIM,))
    return rep(t_wide), rep(t_mid), rep(t_near), rep(t_new)


def _swa_sample_kernel(q_ref, kn_ref, vn_ref, ka_ref, va_ref, kl_ref, vl_ref, kb_ref, vb_ref,
                       tw_ref, tm_ref, tnr_ref, tnew_ref, o_ref, m_s, l_s, acc, *, t_s, n_groups):
    c = pl.program_id(1)
    n_steps = pl.num_programs(1)
    mb = SWA_MB

    @pl.when(c == 0)
    def _():
        m_s[...] = jnp.full(m_s.shape, NEG_BIG, F32)
        l_s[...] = jnp.zeros(l_s.shape, F32)
        acc[...] = jnp.zeros(acc.shape, F32)

    def absorb(slot, q, kt, vt, bias):
        m_run, l_run, a_run = m_s[slot], l_s[slot], acc[slot]
        for lo in range(0, kt.shape[0], SWA_KEYS_PER_PASS):
            hi = min(lo + SWA_KEYS_PER_PASS, kt.shape[0])
            s = jnp.sum(q[None] * kt[lo:hi], axis=-1, keepdims=True) + bias[lo:hi]
            m_new = jnp.maximum(m_run, jnp.max(s, axis=0))
            alpha = jnp.exp2(m_run - m_new)
            p = jnp.exp2(s - m_new[None])
            l_run = alpha * l_run + jnp.sum(p, axis=0)
            a_run = alpha * a_run + jnp.sum(p * vt[lo:hi], axis=0)
            m_run = m_new
        m_s[slot], l_s[slot], acc[slot] = m_run, l_run, a_run

    def query(i, g):
        return q_ref[i, g] * (SWA_SCALE * LOG2_E)

    def wide(i, carry):
        absorb(2 * t_s + i, query(i, 2), ka_ref[:, i], va_ref[:, i], tw_ref[pl.ds(c * mb, mb)])
        return carry

    lax.fori_loop(0, t_s, wide, 0, unroll=True)

    @pl.when(c == 0)
    def _():
        def wide_last(i, carry):
            absorb(2 * t_s + i, query(i, 2), kl_ref[:, i], vl_ref[:, i], tw_ref[n_groups - mb:n_groups])
            return carry

        lax.fori_loop(0, t_s, wide_last, 0, unroll=True)

        def fresh(i, carry):
            for g in range(N_DIL):
                absorb(g * t_s + i, query(i, g), kn_ref[...], vn_ref[...], tnew_ref[g, i])
            return carry

        lax.fori_loop(0, t_s, fresh, 0, unroll=2)

    @pl.when(c == 1)
    def _():
        def mid(i, carry):
            rho, hi = i % 4, i // 4
            kts, vts, bs = [], [], []
            for rr in range(4):
                k_src, v_src = (kl_ref, vl_ref) if rr < 2 else (kb_ref, vb_ref)
                r_idx = rho + 4 * (rr % 2)
                kts.append(k_src[:, r_idx])
                vts.append(v_src[:, r_idx])
                bs.append(tm_ref[hi - rr + 3])
            absorb(t_s + i, query(i, 1), jnp.concatenate(kts, axis=0), jnp.concatenate(vts, axis=0),
                   jnp.concatenate(bs, axis=0))
            return carry

        lax.fori_loop(0, t_s, mid, 0, unroll=2)

    @pl.when(c == n_steps - 1)
    def _():
        near_groups = SWA_BLK // SWA_RES
        n_near = near_groups * (SWA_RES // 2)

        def near(i, carry):
            kts, vts, bs = [], [], []
            for k_src, v_src, off in ((kl_ref, vl_ref, 7), (kb_ref, vb_ref, 7 + SWA_RES // 2)):
                kts.append(k_src[mb - near_groups:mb].reshape(n_near, SWA_HEADS, HEAD_DIM))
                vts.append(v_src[mb - near_groups:mb].reshape(n_near, SWA_HEADS, HEAD_DIM))
                bs += [tnr_ref[pl.ds(SWA_RES * grp + off - i, SWA_RES // 2)] for grp in range(near_groups)]
            absorb(i, query(i, 0), jnp.concatenate(kts, axis=0), jnp.concatenate(vts, axis=0),
                   jnp.concatenate(bs, axis=0))
            return carry

        lax.fori_loop(0, t_s, near, 0, unroll=2)

        for i in range(t_s):
            lse = [m_s[g * t_s + i] + jnp.log2(l_s[g * t_s + i]) for g in range(N_DIL)]
            mx = jnp.maximum(jnp.maximum(lse[0], lse[1]), lse[2])
            e = [jnp.exp2(x - mx) for x in lse]
            num = sum(e[g] * (acc[g * t_s + i] / l_s[g * t_s + i]) for g in range(N_DIL))
            o_ref[i] = num / (e[0] + e[1] + e[2])


def _swa_sample(q_new, k_new, v_new, cache_k, cache_v, tables, n_b, t_s):
    l_cache = cache_k.shape[1]
    n_groups = l_cache // SWA_RES
    n_steps = n_groups // SWA_MB - 1
    assert t_s == SWA_RES // 2 and l_cache % (SWA_RES * SWA_MB) == 0 and l_cache >= SWA_MAX_WINDOW and n_steps >= 3
    ck = cache_k.reshape(n_b, n_groups, SWA_RES, SWA_HEADS, HEAD_DIM)
    cv = cache_v.reshape(n_b, n_groups, SWA_RES, SWA_HEADS, HEAD_DIM)
    half = (SWA_MB, SWA_RES // 2, SWA_HEADS, HEAD_DIM)
    a_spec = pl.BlockSpec((None,) + half, lambda b, c: (b, c, 0, 0, 0))
    last_spec = lambda r: pl.BlockSpec((None,) + half, lambda b, c: (b, n_steps, r, 0, 0))
    new_spec = pl.BlockSpec((None, t_s, SWA_HEADS, HEAD_DIM), lambda b, c: (b, 0, 0, 0))
    const = lambda t: pl.BlockSpec(t.shape, lambda b, c: (0,) * t.ndim)
    slots = N_DIL * t_s
    state = pltpu.VMEM((slots, SWA_HEADS, HEAD_DIM), F32)
    return pl.pallas_call(
        functools.partial(_swa_sample_kernel, t_s=t_s, n_groups=n_groups),
        grid=(n_b, n_steps),
        in_specs=[pl.BlockSpec((None, t_s, N_DIL, SWA_HEADS, HEAD_DIM), lambda b, c: (b, 0, 0, 0, 0)),
                  new_spec, new_spec, a_spec, a_spec, last_spec(0), last_spec(0), last_spec(1), last_spec(1)]
                 + [const(t) for t in tables],
        out_specs=new_spec,
        out_shape=jax.ShapeDtypeStruct((n_b, t_s, SWA_HEADS, HEAD_DIM), F32),
        scratch_shapes=[state, state, state],
        compiler_params=_params("arbitrary", "arbitrary"),
        name="swa_sample",
    )(q_new, k_new, v_new, ck, cv, ck, cv, ck, cv, *tables)


def _swa_mixer(x, n_p, n_b, t_s, cache_k, cache_v, w_in, w_out, rel_bias):
    tm = _pick_tile(x.shape[0], PROJ_TM)
    proj = _matmul(x, w_in, SWA_NQ + 2 * SWA_NKV, tm, 512)
    biases = _group_biases(rel_bias)
    o_p = _swa_prompt(proj, _prompt_bias_tables(biases), n_p)
    k_all = proj[:, SWA_NQ:SWA_NQ + SWA_NKV]
    v_all = proj[:, SWA_NQ + SWA_NKV:]
    per_head = (n_b, t_s, SWA_HEADS, HEAD_DIM)
    o_s = _swa_sample(proj[n_p:, :SWA_NQ].reshape(n_b, t_s, N_DIL, SWA_HEADS, HEAD_DIM),
                      k_all[n_p:].reshape(per_head), v_all[n_p:].reshape(per_head), cache_k, cache_v,
                      _sample_bias_tables(biases, cache_k.shape[1], t_s), n_b, t_s)
    y = _matmul(jnp.concatenate([o_p, o_s.reshape(n_b * t_s, SWA_NKV)], axis=0), w_out, D_MODEL, tm, 512)
    return y, k_all, v_all


def _pad_rows(w, rows):
    return jnp.zeros((rows, w.shape[1]), w.dtype).at[:w.shape[0]].set(w)


def _pad_cols(w, cols):
    return jnp.zeros((w.shape[0], cols), w.dtype).at[:, :w.shape[1]].set(w)


def _gla_mixer(x0, n_p, n_b, t_s, state0, w_in, w_gate, b_gate, norm_w, w_out):
    tm = _pick_tile(x0.shape[0], PROJ_TM)
    proj = _matmul(x0, w_in, GLA_MAIN, tm, 512)
    glow = _matmul(x0, _pad_cols(w_in[:, GLA_MAIN:], LANES), LANES, tm, LANES)
    wg_pad = _pad_rows(w_gate, LANES)
    bg = b_gate.reshape(1, GLA_DK)
    nw = norm_w.reshape(1, GLA_DV_HEAD)
    zero_state = jnp.zeros((1, GLA_HEADS, GLA_DK_HEAD, GLA_DV_HEAD), F32)
    o_p, st_p = _gla(proj, glow, wg_pad, bg, nw, zero_state, n_seq=1, seq_len=n_p, row_start=0,
                     rows_per_step=512, chunk=GLA_CHUNK, mxu_intra=True)
    o_s, st_s = _gla(proj, glow, wg_pad, bg, nw, state0, n_seq=n_b, seq_len=t_s, row_start=n_p,
                     rows_per_step=t_s, chunk=math.gcd(t_s, GLA_CHUNK), mxu_intra=False)
    o = jnp.concatenate([o_p, o_s], axis=0)
    return _matmul(o, w_out, D_MODEL, tm, 512), st_p, st_s


def kernel(x_prompt, x_sample, state_gla, cache_swa_k, cache_swa_v, gla_w_in, gla_w_gate, gla_b_gate, gla_norm_w,
           gla_w_out, swa_w_in, swa_w_out, rel_bias, w_router, b_router, moe_w_gate, moe_w_up, moe_w_down,
           ln1_g, ln1_b, ln2_g, ln2_b):
    n_p = x_prompt.shape[0] * x_prompt.shape[1]
    n_b, t_s = x_sample.shape[0], x_sample.shape[1]
    n_s = n_b * t_s
    assert x_prompt.shape[0] == 1, "one prompt sequence"
    x = jnp.concatenate([x_prompt.reshape(n_p, D_MODEL), x_sample.reshape(n_s, D_MODEL)], axis=0)
    wr_pad = _pad_cols(w_router, LANES)
    br_pad = _pad_cols(b_router.reshape(1, N_EXPERTS), LANES)
    tm_ln = _pick_tile(n_p + n_s, LN_TM)

    gla_p, gla_s, kp, vp, ksm, vsm = [], [], [], [], [], []
    for i in range(DEPTH):
        j = i // 2
        if i % 2 == 0:
            y, st_p, st_s = _gla_mixer(x, n_p, n_b, t_s, state_gla[j], gla_w_in[j], gla_w_gate[j], gla_b_gate[j],
                                       gla_norm_w[j], gla_w_out[j])
            gla_p.append(st_p)
            gla_s.append(st_s)
        else:
            y, k_all, v_all = _swa_mixer(x, n_p, n_b, t_s, cache_swa_k[j], cache_swa_v[j], swa_w_in[j], swa_w_out[j],
                                         rel_bias)
            keep = min(SWA_MAX_WINDOW, n_p)
            kp.append(k_all[n_p - keep:n_p].reshape(1, keep, SWA_HEADS, HEAD_DIM))
            vp.append(v_all[n_p - keep:n_p].reshape(1, keep, SWA_HEADS, HEAD_DIM))
            ksm.append(k_all[n_p:].reshape(n_b, t_s, SWA_HEADS, HEAD_DIM))
            vsm.append(v_all[n_p:].reshape(n_b, t_s, SWA_HEADS, HEAD_DIM))
        x1, idx, gate, counts = _ln_route(x, y, ln1_g[i:i + 1], ln1_b[i:i + 1], wr_pad, br_pad, tm_ln)
        last = i == DEPTH - 1
        x = _moe_ln(x1, idx, gate, counts, moe_w_gate, moe_w_up, moe_w_down, i, ln2_g[i:i + 1], ln2_b[i:i + 1],
                    n_split=n_p if last else None)
    y_p, y_s = x
    return (y_p.reshape(x_prompt.shape), y_s.reshape(x_sample.shape), jnp.stack(gla_p), jnp.stack(gla_s),
            jnp.stack(kp), jnp.stack(vp), jnp.stack(ksm), jnp.stack(vsm))
```

```python
import functools
import math

import numpy as np
import jax
import jax.numpy as jnp
from jax import lax
from jax.experimental import pallas as pl
from jax.experimental.pallas import tpu as pltpu

F32 = jnp.float32
BF16 = jnp.bfloat16
I32 = jnp.int32

D_MODEL = 2048
DEPTH = 2
GLA_HEADS = 4
GLA_DK_HEAD = 256
GLA_DV_HEAD = 512
GLA_DK = GLA_HEADS * GLA_DK_HEAD
GLA_DV = GLA_HEADS * GLA_DV_HEAD
GLA_GATE_RANK = 16
GLA_GATE_TAU = 16.0
GLA_CHUNK = 64
GLA_SUB = 16
GLA_SEQS_PER_STEP = 4
GLA_MAIN = 2 * GLA_DK + 2 * GLA_DV
RMS_EPS = 1e-6
HEAD_DIM = 128
SWA_HEADS = 16
DIL_GROUPS = ((128, 1), (512, 4), (2048, 16))
N_DIL = len(DIL_GROUPS)
SWA_NQ = N_DIL * SWA_HEADS * HEAD_DIM
SWA_NKV = SWA_HEADS * HEAD_DIM
SWA_MAX_WINDOW = 2048
NUM_BUCKETS = 32
MAX_DISTANCE = 2048
N_EXPERTS = 16
N_EXPERT_GROUPS = 4
EXPERTS_PER_GROUP = 4
D_EXPERT = 1024
DEEPNORM_ALPHA = (2 * DEPTH) ** 0.25
LN_EPS = 1e-5

LANES = 128
SUBLANES = 8
VMEM_LIMIT_BYTES = 56 * 1024 * 1024
NEG_BIG = -1e30


PROJ_TM = 1024
LN_TM = 256


def _pick_tile(n, preferred):
    t = preferred
    while t > SUBLANES and n % t:
        t //= 2
    assert n % t == 0, (n, t)
    return t


def _params(*sem):
    return pltpu.CompilerParams(dimension_semantics=sem, vmem_limit_bytes=VMEM_LIMIT_BYTES)


def _dot(a, b):
    return jnp.dot(a, b, preferred_element_type=F32)


def _dot_nt(a, b):
    return lax.dot_general(a, b, (((1,), (1,)), ((), ())), preferred_element_type=F32)


def _dot_tn(a, b):
    return lax.dot_general(a, b, (((0,), (0,)), ((), ())), preferred_element_type=F32)


def _split3(a):
    hi = a.astype(BF16)
    r1 = a - hi.astype(F32)
    mid = r1.astype(BF16)
    lo = (r1 - mid.astype(F32)).astype(BF16)
    return hi, mid, lo


def _row_segments(x):
    segs = tuple(x) if isinstance(x, (tuple, list)) else (x,)
    assert len(segs) in (1, 2)
    return segs


def _segment_specs(segs, tm):
    width = segs[0].shape[1]
    assert all(s.shape[0] % tm == 0 and s.shape[1] == width for s in segs)
    first = segs[0].shape[0] // tm
    if len(segs) == 1:
        return [pl.BlockSpec((tm, width), lambda i, *_: (i, 0))], first
    return [pl.BlockSpec((tm, width), lambda i, *_: (jnp.minimum(i, first - 1), 0)),
            pl.BlockSpec((tm, width), lambda i, *_: (jnp.maximum(i - first, 0), 0))], first


def _segment_tile(x_refs, first_tiles):
    if len(x_refs) == 1:
        return x_refs[0][...]
    return jnp.where(pl.program_id(0) < first_tiles, x_refs[0][...], x_refs[1][...])


def _mm_kernel(*refs, first_tiles):
    *x_refs, w_ref, o_ref, xb_ref = refs

    @pl.when(pl.program_id(1) == 0)
    def _():
        xb_ref[...] = _segment_tile(x_refs, first_tiles).astype(BF16)

    o_ref[...] = _dot(xb_ref[...], w_ref[...].astype(BF16)).astype(o_ref.dtype)


def _matmul(x, w, n_out, tm, tn):
    segs = _row_segments(x)
    m = sum(s.shape[0] for s in segs)
    k = segs[0].shape[1]
    assert n_out % tn == 0 and w.shape[0] == k
    x_specs, first = _segment_specs(segs, tm)
    return pl.pallas_call(
        functools.partial(_mm_kernel, first_tiles=first),
        grid=(m // tm, n_out // tn),
        in_specs=x_specs + [pl.BlockSpec((k, tn), lambda i, j: (0, j))],
        out_specs=pl.BlockSpec((tm, tn), lambda i, j: (i, j)),
        out_shape=jax.ShapeDtypeStruct((m, n_out), F32),
        scratch_shapes=[pltpu.VMEM((tm, k), BF16)],
        compiler_params=_params("arbitrary", "arbitrary"),
        name="dense_proj",
    )(*segs, w)


def _gla_kernel(q_ref, k_ref, v_ref, r_ref, gl_ref, wg_ref, bg_ref, nw_ref, s0_ref,
                o_ref, sf_ref, state, bsc, *, chunk, n_chunks, seqs, mxu_intra):
    c_sz = chunk
    r_sz = chunk * n_chunks
    t = pl.program_id(2)

    @pl.when(t == 0)
    def _():
        state[...] = s0_ref[...]

    rowi = lax.broadcasted_iota(I32, (c_sz, GLA_DK_HEAD), 0)
    tri = (lax.broadcasted_iota(I32, (c_sz, c_sz), 0) >= lax.broadcasted_iota(I32, (c_sz, c_sz), 1)).astype(BF16)
    ones = jnp.ones((c_sz, LANES), BF16)
    wg =wg_ref[...].astype(BF16)

    def one_chunk(sb, c):
        rows = pl.ds(pl.multiple_of(sb * r_sz + c * c_sz, c_sz), c_sz)
        q = q_ref[rows, :] * (GLA_DK_HEAD ** -0.5)
        k = k_ref[rows, :]
        v = v_ref[rows, :]
        z = _dot(gl_ref[rows, :].astype(BF16), wg) + bg_ref[...]
        la = (jnp.minimum(z, 0.0) - jnp.log(1.0 + jnp.exp(-jnp.abs(z)))) * (1.0 / GLA_GATE_TAU)
        l_hi, l_mid, l_lo = _split3(la)
        b = _dot(tri, l_hi) + (_dot(tri, l_mid) + _dot(tri, l_lo))
        dcol = _dot_tn(l_hi, ones) + (_dot_tn(l_mid, ones) + _dot_tn(l_lo, ones))
        b_last = b[c_sz - 1:c_sz, :]

        s_old = state[sb]
        o = _dot((q * jnp.exp(b)).astype(BF16), s_old.astype(BF16))

        if mxu_intra:
            n_sub = c_sz // GLA_SUB
            parts = [jnp.zeros((GLA_SUB, c_sz), F32)]
            colc = lax.broadcasted_iota(I32, (GLA_SUB, c_sz), 1)
            for blk in range(1, n_sub):
                lo = blk * GLA_SUB
                ref_b = b[lo - 1:lo, :]
                q_rel = q[lo:lo + GLA_SUB, :] * jnp.exp(b[lo:lo + GLA_SUB, :] - ref_b)
                k_rel = k * jnp.exp(jnp.minimum(ref_b - b, 0.0))
                s_blk = _dot_nt(q_rel.astype(BF16), k_rel.astype(BF16))
                parts.append(jnp.where(colc < lo, s_blk, 0.0))
            att = jnp.concatenate(parts, axis=0)

            b3 = b.reshape(n_sub, GLA_SUB, GLA_DK_HEAD)
            q3 = q.reshape(n_sub, GLA_SUB, GLA_DK_HEAD)
            k3 = k.reshape(n_sub, GLA_SUB, GLA_DK_HEAD)
            row3 = lax.broadcasted_iota(I32, b3.shape, 1)
            blk_lane0 = (lax.broadcasted_iota(I32, (c_sz, c_sz), 0) // GLA_SUB) * GLA_SUB
            lane_c = lax.broadcasted_iota(I32, (c_sz, c_sz), 1)
            for jj in range(GLA_SUB):
                e = jnp.exp(jnp.where(row3 >= jj, b3 - b3[:, jj:jj + 1, :], -jnp.inf))
                col = jnp.sum(q3 * e * k3[:, jj:jj + 1, :], axis=-1, keepdims=True).reshape(c_sz, 1)
                att = jnp.where(lane_c == blk_lane0 + jj, col, att)
            o = o + _dot(att.astype(BF16), v.astype(BF16))
        else:
            bsc[sb] = b
            qk_base = sb * r_sz + c * c_sz

            def jbody(j, acc):
                kj = k_ref[pl.ds(qk_base + j, 1), :]
                vj = v_ref[pl.ds(qk_base + j, 1), :]
                e = jnp.exp(jnp.where(rowi >= j, b - bsc[sb, pl.ds(j, 1), :], -jnp.inf))
                return acc + jnp.sum(q * e * kj, axis=-1, keepdims=True) * vj

            o = lax.fori_loop(0, c_sz, jbody, o, unroll=True)

        kt = (k * jnp.exp(b_last - b)).astype(BF16)
        decay = jnp.exp(dcol)
        ds = _dot_tn(kt, v.astype(BF16))
        state[sb] = s_old * jnp.tile(decay, (1, GLA_DV_HEAD // LANES)) + ds

        o = o * lax.rsqrt(jnp.mean(o * o, axis=-1, keepdims=True) + RMS_EPS) * nw_ref[...]
        r = r_ref[rows, :]
        o_ref[rows, :] = o * (r / (1.0 + jnp.exp(-r)))

    for sb in range(seqs):
        if n_chunks == 1:
            one_chunk(sb, 0)
        else:
            def body(c, carry, sb=sb):
                one_chunk(sb, c)
                return carry
            lax.fori_loop(0, n_chunks, body, 0, unroll=2)

    @pl.when(t == pl.num_programs(2) - 1)
    def _():
        sf_ref[...] = state[...]


def _gla(proj, glow, w_gate_pad, b_gate, norm_w, s0, *, n_seq, seq_len, row_start, rows_per_step, chunk,
         mxu_intra):
    r_sz = rows_per_step
    steps = seq_len // r_sz
    seqs = math.gcd(n_seq, GLA_SEQS_PER_STEP) if steps == 1 else 1
    blk_rows = seqs * r_sz
    assert seq_len % r_sz == 0 and r_sz % chunk == 0 and row_start % blk_rows == 0
    rb0 = row_start // blk_rows
    kq, kv = GLA_DK // GLA_DK_HEAD, GLA_DK // GLA_DV_HEAD

    def rowblk(b, t):
        return rb0 + b * steps + t

    st_spec = pl.BlockSpec((seqs, None, GLA_DK_HEAD, GLA_DV_HEAD), lambda b, h, t: (b, h, 0, 0))
    kern = functools.partial(_gla_kernel, chunk=chunk, n_chunks=r_sz // chunk, seqs=seqs, mxu_intra=mxu_intra)
    return pl.pallas_call(
        kern,
        grid=(n_seq // seqs, GLA_HEADS, steps),
        in_specs=[
            pl.BlockSpec((blk_rows, GLA_DK_HEAD), lambda b, h, t: (rowblk(b, t), h)),
            pl.BlockSpec((blk_rows, GLA_DK_HEAD), lambda b, h, t: (rowblk(b, t), kq + h)),
            pl.BlockSpec((blk_rows, GLA_DV_HEAD), lambda b, h, t: (rowblk(b, t), 2 * kv + h)),
            pl.BlockSpec((blk_rows, GLA_DV_HEAD), lambda b, h, t: (rowblk(b, t), 2 * kv + GLA_HEADS + h)),
            pl.BlockSpec((blk_rows, LANES), lambda b, h, t: (rowblk(b, t), 0)),
            pl.BlockSpec((LANES, GLA_DK_HEAD), lambda b, h, t: (0, h)),
            pl.BlockSpec((1, GLA_DK_HEAD), lambda b, h, t: (0, h)),
            pl.BlockSpec((1, GLA_DV_HEAD), lambda b, h, t: (0, 0)),
            st_spec,
        ],
        out_specs=[
            pl.BlockSpec((blk_rows, GLA_DV_HEAD), lambda b, h, t: (b * steps + t, h)),
            st_spec,
        ],
        out_shape=[jax.ShapeDtypeStruct((n_seq * seq_len, GLA_DV), F32),
                   jax.ShapeDtypeStruct((n_seq, GLA_HEADS, GLA_DK_HEAD, GLA_DV_HEAD), F32)],
        scratch_shapes=[pltpu.VMEM((seqs, GLA_DK_HEAD, GLA_DV_HEAD), F32),
                        pltpu.VMEM((seqs, chunk, GLA_DK_HEAD), F32)],
        compiler_params=_params("arbitrary", "arbitrary", "arbitrary"),
        name="gla_chunked",
    )(proj, proj, proj, proj, glow, w_gate_pad, b_gate, norm_w, s0)


def _layer_norm(x, g, b):
    mu = jnp.mean(x, axis=-1, keepdims=True)
    xc = x - mu
    var = jnp.mean(xc * xc, axis=-1, keepdims=True)
    return xc * lax.rsqrt(var + LN_EPS) * g + b


def _route(x, wr_ref, br_ref, idx_ref, gate_ref, cnt_ref, seen):
    tm = x.shape[0]
    xh, xm, xl = _split3(x)
    wh, wm, wl = _split3(wr_ref[...])
    logits = _dot(xh, wh) + ((_dot(xh, wm) + _dot(xm, wh)) + ((_dot(xm, wm) + _dot(xh, wl)) + _dot(xl, wh)))
    scores = 1.0 / (1.0 + jnp.exp(-logits))
    lane = lax.broadcasted_iota(I32, (tm, LANES), 1).astype(F32)
    sel = scores + br_ref[...]
    big = float(LANES)
    best = None
    for g in range(N_EXPERT_GROUPS):
        in_g = (lane >= g * EXPERTS_PER_GROUP) & (lane < (g + 1) * EXPERTS_PER_GROUP)
        vg = jnp.where(in_g, sel, -jnp.inf)
        m1 = jnp.max(vg, axis=-1, keepdims=True)
        i1 = jnp.min(jnp.where(vg == m1, lane, big), axis=-1, keepdims=True)
        vg2 = jnp.where(lane == i1, -jnp.inf, vg)
        m2 = jnp.max(vg2, axis=-1, keepdims=True)
        i2 = jnp.min(jnp.where(vg2 == m2, lane, big), axis=-1, keepdims=True)
        gs = m1 + m2
        if best is None:
            best, b1, b2 = gs, i1, i2
        else:
            better = gs > best
            best = jnp.where(better, gs, best)
            b1 = jnp.where(better, i1, b1)
            b2 = jnp.where(better, i2, b2)
    w1 = jnp.sum(jnp.where(lane == b1, scores, 0.0), axis=-1, keepdims=True)
    w2 = jnp.sum(jnp.where(lane == b2, scores, 0.0), axis=-1, keepdims=True)
    den = w1 + w2
    picked = jnp.logical_or(lane == b1, lane == b2).astype(F32)
    earlier = (lax.broadcasted_iota(I32, (tm, tm), 0) > lax.broadcasted_iota(I32, (tm, tm), 1)).astype(BF16)
    before = _dot(earlier, picked.astype(BF16)) + seen[...]
    r1 = jnp.sum(jnp.where(lane == b1, before, 0.0), axis=-1, keepdims=True)
    r2 = jnp.sum(jnp.where(lane == b2, before, 0.0), axis=-1, keepdims=True)
    seen[...] = seen[...] + jnp.sum(picked, axis=0, keepdims=True)
    cnt_ref[...] = jnp.broadcast_to(seen[...], cnt_ref.shape)
    idx_ref[...] = jnp.where(lane == 0.0, b1, jnp.where(lane == 1.0, b2, jnp.where(lane == 2.0, r1, jnp.where(
        lane == 3.0, r2, 0.0)))).astype(I32)
    gate_ref[...] = jnp.where(lane == 0.0, w1 / den, jnp.where(lane == 1.0, w2 / den, 0.0))


def _ln_route_kernel(*refs, first_tiles):
    *x_refs, y_ref, g_ref, b_ref, wr_ref, br_ref, o_ref, idx_ref, gate_ref, cnt_ref, seen = refs

    @pl.when(pl.program_id(0) == 0)
    def _():
        seen[...] = jnp.zeros(seen.shape, F32)

    x = _segment_tile(x_refs, first_tiles)
    x1 = _layer_norm(DEEPNORM_ALPHA * x + y_ref[...], g_ref[...], b_ref[...])
    o_ref[...] = x1
    _route(x1, wr_ref, br_ref, idx_ref, gate_ref, cnt_ref, seen)


def _ln_route(x, y, g, b, wr_pad, br_pad, tm):
    segs = _row_segments(x)
    n, d = y.shape
    row = lambda i: (i, 0)
    fixed = lambda i: (0, 0)
    x_specs, first = _segment_specs(segs, tm)
    return pl.pallas_call(
        functools.partial(_ln_route_kernel, first_tiles=first),
        grid=(n // tm,),
        in_specs=x_specs + [pl.BlockSpec((tm, d), row),
                  pl.BlockSpec((1, d), fixed), pl.BlockSpec((1, d), fixed),
                  pl.BlockSpec((d, LANES), fixed), pl.BlockSpec((1, LANES), fixed)],
        out_specs=[pl.BlockSpec((tm, d), row), pl.BlockSpec((tm, LANES), row), pl.BlockSpec((tm, LANES), row),
                   pl.BlockSpec((SUBLANES, LANES), fixed)],
        out_shape=[jax.ShapeDtypeStruct((n, d), F32), jax.ShapeDtypeStruct((n, LANES), I32),
                   jax.ShapeDtypeStruct((n, LANES), F32), jax.ShapeDtypeStruct((SUBLANES, LANES), F32)],
        scratch_shapes=[pltpu.VMEM((1, LANES), F32)],
        compiler_params=_params("arbitrary"),
        name="deepnorm_ln_router",
    )(*segs, y, g, b, wr_pad, br_pad)


def _gmm1_kernel(src_ref, te_ref, ch_ref, nu_ref, x_hbm, wg_ref, wu_ref, h_ref, xbuf, wgb, wub, sems, *, tm):
    i = pl.program_id(0)
    n_used = nu_ref[0]

    def row_copy(tile, r, slot):
        return pltpu.make_async_copy(x_hbm.at[pl.ds(src_ref[tile * tm + r], 1), :],
                                     xbuf.at[slot, pl.ds(r, 1), :], sems.at[slot])

    def fetch(tile, slot):
        def issue(r, c):
            row_copy(tile, r, slot).start()
            return c
        lax.fori_loop(0, tm, issue, 0, unroll=DMA_UNROLL)

    def drain(tile, slot):
        def wait_row(r, c):
            row_copy(tile, r, slot).wait()
            return c
        lax.fori_loop(0, tm, wait_row, 0, unroll=DMA_UNROLL)

    @pl.when(i == 0)
    def _():
        fetch(0, 0)

    @pl.when(i < n_used)
    def _():
        slot = i % 2
        drain(i, slot)

        @pl.when(ch_ref[i] == 1)
        def _():
            wgb[...] = wg_ref[...].astype(BF16)
            wub[...] = wu_ref[...].astype(BF16)

        x = xbuf[slot].astype(BF16)
        a = _dot(x, wgb[...])
        for r in range(tm):
            row_copy(i + 1, r, 1 - slot).start()
        u = _dot(x, wub[...])
        h_ref[...] = ((a / (1.0 + jnp.exp(-a))) * u).astype(BF16)

    @pl.when(i >= n_used)
    def _():
        @pl.when(i == n_used)
        def _():
            drain(i, i % 2)

        h_ref[...] = jnp.zeros(h_ref.shape, h_ref.dtype)


def _gmm1(x, src_row, w_gate, w_up, layer, meta, tm):
    te, _, ch, nu = meta
    n_tiles = src_row.shape[0] // tm
    d = x.shape[1]
    f = w_gate.shape[-1]
    wspec = pl.BlockSpec((None, None, d, f), lambda i, src, te, ch, nu: (layer, te[i], 0, 0))
    return pl.pallas_call(
        functools.partial(_gmm1_kernel, tm=tm),
        grid_spec=pltpu.PrefetchScalarGridSpec(
            num_scalar_prefetch=4,
            grid=(n_tiles,),
            in_specs=[pl.BlockSpec(memory_space=pl.ANY), wspec, wspec],
            out_specs=pl.BlockSpec((tm, f), lambda i, src, te, ch, nu: (i, 0)),
            scratch_shapes=[pltpu.VMEM((2, tm, d), F32), pltpu.VMEM((d, f), BF16), pltpu.VMEM((d, f), BF16),
                            pltpu.SemaphoreType.DMA((2,))],
        ),
        out_shape=jax.ShapeDtypeStruct((n_tiles * tm, f), BF16),
        compiler_params=_params("arbitrary"),
        name="moe_gather_gate_up",
    )(src_row, te, ch, nu, x, w_gate, w_up)


def _gmm2_kernel(te_ref, tb_ref, ch_ref, nu_ref, h_ref, wd_ref, y_ref, wdb):
    i = pl.program_id(1)

    @pl.when(i < nu_ref[0])
    def _():
        @pl.when(ch_ref[i] == 1)
        def _():
            wdb[...] = wd_ref[...].astype(BF16)

        y_ref[...] = _dot(h_ref[...], wdb[...])

    @pl.when(i >= nu_ref[0])
    def _():
        y_ref[...] = jnp.zeros(y_ref.shape, y_ref.dtype)


def _gmm2(h, w_down, layer, meta, tm, tn):
    te, tb, ch, nu = meta
    n_tiles = h.shape[0] // tm
    f = h.shape[1]
    d = w_down.shape[-1]
    return pl.pallas_call(
        _gmm2_kernel,
        grid_spec=pltpu.PrefetchScalarGridSpec(
            num_scalar_prefetch=4,
            grid=(d // tn, n_tiles),
            in_specs=[pl.BlockSpec((tm, f), lambda j, i, te, tb, ch, nu: (tb[i], 0)),
                      pl.BlockSpec((None, None, f, tn), lambda j, i, te, tb, ch, nu: (layer, te[i], 0, j))],
            out_specs=pl.BlockSpec((tm, tn), lambda j, i, te, tb, ch, nu: (i, j)),
            scratch_shapes=[pltpu.VMEM((f, tn), BF16)],
        ),
        out_shape=jax.ShapeDtypeStruct((n_tiles * tm, d), F32),
        compiler_params=_params("arbitrary", "arbitrary"),
        name="moe_down",
    )(te, tb, ch, nu, h, w_down)


def _combine_ln_kernel(pos_ref, y_hbm, x_ref, gate_ref, g_ref, b_ref, *rest, tm, n_split_tiles):
    if n_split_tiles is None:
        (o_ref, ybuf, sems) = rest
    else:
        (op_ref, os_ref, ybuf, sems) = rest
    i = pl.program_id(0)

    def row_copy(tile, r, k, slot):
        return pltpu.make_async_copy(y_hbm.at[pl.ds(pos_ref[2 * (tile * tm + r) + k], 1), :],
                                     ybuf.at[slot, k, pl.ds(r, 1), :], sems.at[slot])

    def fetch(tile, slot):
        def issue(r, c):
            row_copy(tile, r, 0, slot).start()
            row_copy(tile, r, 1, slot).start()
            return c
        lax.fori_loop(0, tm, issue, 0, unroll=DMA_UNROLL)

    @pl.when(i == 0)
    def _():
        fetch(0, 0)

    @pl.when(i + 1 < pl.num_programs(0))
    def _():
        fetch(i + 1, (i + 1) % 2)

    slot = i % 2

    def drain(r, c):
        row_copy(i, r, 0, slot).wait()
        row_copy(i, r, 1, slot).wait()
        return c

    lax.fori_loop(0, tm, drain, 0, unroll=DMA_UNROLL)
    gate = gate_ref[...]
    moe = gate[:, 0:1] * ybuf[slot, 0] + gate[:, 1:2] * ybuf[slot, 1]
    out = _layer_norm(DEEPNORM_ALPHA * x_ref[...] + moe, g_ref[...], b_ref[...])
    if n_split_tiles is None:
        o_ref[...] = out
    else:
        @pl.when(i < n_split_tiles)
        def _():
            op_ref[...] = out

        @pl.when(i >= n_split_tiles)
        def _():
            os_ref[...] = out


def _combine_ln(y_sorted, pos, x, gate, g, b, tm, n_split=None):
    n, d = x.shape
    row = lambda i, pos: (i, 0)
    fixed = lambda i, pos: (0, 0)
    if n_split is None:
        nst = None
        out_specs = pl.BlockSpec((tm, d), row)
        out_shape = jax.ShapeDtypeStruct((n, d), F32)
    else:
        nst = n_split // tm
        out_specs = [pl.BlockSpec((tm, d), lambda i, pos: (jnp.minimum(i, nst - 1), 0)),
                     pl.BlockSpec((tm, d), lambda i, pos: (jnp.maximum(i - nst, 0), 0))]
        out_shape = [jax.ShapeDtypeStruct((n_split, d), F32), jax.ShapeDtypeStruct((n - n_split, d), F32)]
    return pl.pallas_call(
        functools.partial(_combine_ln_kernel, tm=tm, n_split_tiles=nst),
        grid_spec=pltpu.PrefetchScalarGridSpec(
            num_scalar_prefetch=1,
            grid=(n // tm,),
            in_specs=[pl.BlockSpec(memory_space=pl.ANY), pl.BlockSpec((tm, d), row),
                      pl.BlockSpec((tm, LANES), row), pl.BlockSpec((1, d), fixed), pl.BlockSpec((1, d), fixed)],
            out_specs=out_specs,
            scratch_shapes=[pltpu.VMEM((2, 2, tm, d), F32), pltpu.SemaphoreType.DMA((2,))],
        ),
        out_shape=out_shape,
        compiler_params=_params("arbitrary"),
        name="moe_combine_ln",
    )(pos, y_sorted, x, gate, g, b)


def _dispatch_meta(idx, counts, tm, n_tiles):
    n = idx.shape[0]
    e_flat = idx[:, :2].reshape(-1)
    rank = idx[:, 2:4].reshape(-1)
    counts = counts[0, :N_EXPERTS].astype(I32)
    ptiles = (counts + tm - 1) // tm
    tile_end = jnp.cumsum(ptiles)
    n_used = tile_end[-1]
    row_off = (tile_end - ptiles) * tm
    pos = row_off[e_flat] + rank
    tok = jnp.arange(2 * n, dtype=I32) // 2
    src_row = jnp.zeros((n_tiles * tm,), I32).at[pos].set(tok)
    tile_id = jnp.minimum(jnp.arange(n_tiles, dtype=I32), n_used - 1)
    tile_expert = jnp.sum((tile_id[:, None] >= tile_end[None, :]).astype(I32), axis=1)
    changed = jnp.concatenate([jnp.ones((1,), I32), (tile_expert[1:] != tile_expert[:-1]).astype(I32)])
    nu = n_used.reshape(1).astype(I32)
    return src_row, pos.astype(I32), (tile_expert, tile_id.astype(I32), changed, nu)


DMA_UNROLL = 8
MOE_TM = 256
MOE_TN = 2048


def _moe_ln(x1, idx, gate, counts, w_gate, w_up, w_down, layer, g, b, n_split=None):
    n = x1.shape[0]
    n_tiles = -(-2 * n // MOE_TM) + N_EXPERTS + 1
    src_row, pos, meta = _dispatch_meta(idx, counts, MOE_TM, n_tiles)
    h = _gmm1(x1, src_row, w_gate, w_up, layer, meta, MOE_TM)
    y = _gmm2(h, w_down, layer, meta, MOE_TM, MOE_TN)
    tm = _pick_tile(n if n_split is None else math.gcd(n, n_split), LN_TM)
    return _combine_ln(y, pos, x1, gate, g, b, tm, n_split)


SWA_BLK = 128
SWA_SUPER = SWA_BLK * 16
SWA_SCALE = HEAD_DIM ** -0.5
SWA_TOEPLITZ_PERIOD = 4 * SWA_BLK


def _t5_bucket(dist):
    max_exact = NUM_BUCKETS // 2
    d = jnp.maximum(dist.astype(F32), 1.0)
    large = max_exact + (jnp.log(d / max_exact) / math.log(MAX_DISTANCE / max_exact)
                         * (NUM_BUCKETS - max_exact)).astype(I32)
    large = jnp.minimum(large, NUM_BUCKETS - 1)
    return jnp.where(dist < max_exact, dist, large)


def _group_biases(rel_bias):
    out = []
    for g, (w, d) in enumerate(DIL_GROUPS):
        dist = d * jnp.arange(w // d + 1, dtype=I32)
        b = rel_bias[_t5_bucket(dist)]
        out.append(b[:, g * SWA_HEADS:(g + 1) * SWA_HEADS].T.astype(F32))
    return out


def _prompt_bias_tables(biases):
    b_all = jnp.concatenate(biases, axis=0)
    mask = jnp.full((b_all.shape[0], SWA_TOEPLITZ_PERIOD - SWA_BLK - 1), NEG_BIG, F32)
    return jnp.concatenate([b_all[:, ::-1], mask], axis=1)[:, None, :]


def _swa_prompt_kernel(q0_ref, q1_ref, q2_ref, kc_ref, kp_ref, vc_ref, vp_ref, t0_ref, t1_ref, t2_ref,
                       o_ref, kk, vv, og, ls):
    s = pl.program_id(1)
    sb = SWA_SUPER
    kk[0:sb, :] = kp_ref[...]
    kk[sb:2 * sb, :] = kc_ref[...]
    vv[0:sb, :] = vp_ref[...]
    vv[sb:2 * sb, :] = vc_ref[...]
    col = lax.broadcasted_iota(I32, (SWA_BLK, 2 * SWA_BLK), 1)
    before_start = col < SWA_BLK

    for g, (_, d) in enumerate(DIL_GROUPS):
        q_ref = (q0_ref, q1_ref, q2_ref)[g]
        u_row = jnp.broadcast_to((t0_ref, t1_ref, t2_ref)[g][...], (SWA_BLK, SWA_TOEPLITZ_PERIOD))
        tab = pltpu.roll(u_row, 0, 1, stride=1, stride_axis=0)[:, :2 * SWA_BLK]
        n_mb = sb // (SWA_BLK * d)

        def block(idx, carry, q_ref=q_ref, tab=tab, d=d, n_mb=n_mb, g=g):
            r = idx // n_mb
            mb = idx % n_mb
            start = r + d * SWA_BLK * mb
            kstart = sb + start - d * SWA_BLK
            if d == 1:
                rows_q, rows_k = pl.ds(start, SWA_BLK), pl.ds(kstart, 2 * SWA_BLK)
            else:
                rows_q, rows_k = pl.ds(start, SWA_BLK, stride=d), pl.ds(kstart, 2 * SWA_BLK, stride=d)
            q = q_ref[rows_q, :].astype(BF16)
            k = kk[rows_k, :].astype(BF16)
            v = vv[rows_k, :].astype(BF16)
            lg = _dot_nt(q, k) * SWA_SCALE + tab
            no_prev = jnp.logical_and(s == 0, mb == 0)
            lg = jnp.where(jnp.logical_and(no_prev, before_start), NEG_BIG, lg)
            m = jnp.max(lg, axis=-1, keepdims=True)
            p = jnp.exp(lg - m)
            l = jnp.sum(p, axis=-1, keepdims=True)
            og[g, rows_q, :] = _dot(p.astype(BF16), v) / l
            ls[g, rows_q, :] = jnp.broadcast_to(m + jnp.log(l), (SWA_BLK, HEAD_DIM))
            return carry

        lax.fori_loop(0, sb // SWA_BLK, block, 0, unroll=8)

    l0, l1, l2 = ls[0], ls[1], ls[2]
    mx = jnp.maximum(jnp.maximum(l0, l1), l2)
    e0, e1, e2 = jnp.exp(l0 - mx), jnp.exp(l1 - mx), jnp.exp(l2 - mx)
    o_ref[...] = (e0 * og[0] + e1 * og[1] + e2 * og[2]) / (e0 + e1 + e2)


def _swa_prompt(proj, tables, n_p):
    sb = SWA_SUPER
    assert n_p % sb == 0
    kcol, vcol = SWA_NQ // HEAD_DIM, (SWA_NQ + SWA_NKV) // HEAD_DIM
    blk = lambda f: pl.BlockSpec((sb, HEAD_DIM), f)
    tspec = lambda g: pl.BlockSpec((None, 1, SWA_TOEPLITZ_PERIOD), lambda h, s: (g * SWA_HEADS + h, 0, 0))
    prev = lambda s: jnp.maximum(s - 1, 0)
    return pl.pallas_call(
        _swa_prompt_kernel,
        grid=(SWA_HEADS, n_p // sb),
        in_specs=[blk(lambda h, s: (s, h)), blk(lambda h, s: (s, SWA_HEADS + h)), blk(lambda h, s: (s, 2 * SWA_HEADS + h)),
                  blk(lambda h, s: (s, kcol + h)), blk(lambda h, s: (prev(s), kcol + h)),
                  blk(lambda h, s: (s, vcol + h)), blk(lambda h, s: (prev(s), vcol + h)),
                  tspec(0), tspec(1), tspec(2)],
        out_specs=blk(lambda h, s: (s, h)),
        out_shape=jax.ShapeDtypeStruct((n_p, SWA_NKV), F32),
        scratch_shapes=[pltpu.VMEM((2 * sb, HEAD_DIM), F32), pltpu.VMEM((2 * sb, HEAD_DIM), F32),
                        pltpu.VMEM((N_DIL, sb, HEAD_DIM), F32), pltpu.VMEM((N_DIL, sb, HEAD_DIM), F32)],
        compiler_params=_params("arbitrary", "arbitrary"),
        name="swa_prompt",
    )(proj, proj, proj, proj, proj, proj, proj, tables, tables, tables)


SWA_MB = 32
SWA_RES = 16
SWA_KEYS_PER_PASS = 16
LOG2_E = math.log2(math.e)


def _sample_bias_tables(biases, l_cache, t_s):
    b_near, b_mid, b_wide = biases
    n_groups = l_cache // SWA_RES
    neg = jnp.full((SWA_HEADS, 1), NEG_BIG, F32)

    def pick(b, j):
        j = np.asarray(j)
        ok = (j >= 0) & (j <= SWA_BLK)
        vals = jnp.concatenate([b, neg], axis=1)[:, np.where(ok, j, SWA_BLK + 1)]
        return jnp.moveaxis(vals, 0, -1)

    t_wide = pick(b_wide, n_groups - np.arange(n_groups))
    variant = np.arange(5)[:, None]
    t_mid = pick(b_mid, (SWA_BLK - 3 + variant) - 4 * np.arange(SWA_MB)[None, :])
    t_near = pick(b_near, SWA_BLK + 7 - np.arange(SWA_BLK + 7))
    d_new = np.arange(t_s)[:, None] - np.arange(t_s)[None, :]
    t_new = jnp.stack([pick(b, np.where((d_new >= 0) & (d_new % d == 0), d_new // d, -1))
                       for b, (_, d) in zip(biases, DIL_GROUPS)], axis=0)
    rep = lambda t: jnp.broadcast_to((t * LOG2_E)[..., None], t.shape + (HEAD_DIM,))
    return rep(t_wide), rep(t_mid), rep(t_near), rep(t_new)


def _swa_sample_kernel(q_ref, kn_ref, vn_ref, ka_ref, va_ref, kl_ref, vl_ref, kb_ref, vb_ref,
                       tw_ref, tm_ref, tnr_ref, tnew_ref, o_ref, m_s, l_s, acc, *, t_s, n_groups):
    c = pl.program_id(1)
    n_steps = pl.num_programs(1)
    mb = SWA_MB

    @pl.when(c == 0)
    def _():
        m_s[...] = jnp.full(m_s.shape, NEG_BIG, F32)
        l_s[...] = jnp.zeros(l_s.shape, F32)
        acc[...] = jnp.zeros(acc.shape, F32)

    def absorb(slot, q, kt, vt, bias):
        m_run, l_run, a_run = m_s[slot], l_s[slot], acc[slot]
        for lo in range(0, kt.shape[0], SWA_KEYS_PER_PASS):
            hi = min(lo + SWA_KEYS_PER_PASS, kt.shape[0])
            s = jnp.sum(q[None] * kt[lo:hi], axis=-1, keepdims=True) + bias[lo:hi]
            m_new = jnp.maximum(m_run, jnp.max(s, axis=0))
            alpha = jnp.exp2(m_run - m_new)
            p = jnp.exp2(s - m_new[None])
            l_run = alpha * l_run + jnp.sum(p, axis=0)
            a_run = alpha * a_run + jnp.sum(p * vt[lo:hi], axis=0)
            m_run = m_new
        m_s[slot], l_s[slot], acc[slot] = m_run, l_run, a_run

    def query(i, g):
        return q_ref[i, g] * (SWA_SCALE * LOG2_E)

    def wide(i, carry):
        absorb(2 * t_s + i, query(i, 2), ka_ref[:, i], va_ref[:, i], tw_ref[pl.ds(c * mb, mb)])
        return carry

    lax.fori_loop(0, t_s, wide, 0, unroll=True)

    @pl.when(c == 0)
    def _():
        def wide_last(i, carry):
            absorb(2 * t_s + i, query(i, 2), kl_ref[:, i], vl_ref[:, i], tw_ref[n_groups - mb:n_groups])
            return carry

        lax.fori_loop(0, t_s, wide_last, 0, unroll=True)

        def fresh(i, carry):
            for g in range(N_DIL):
                absorb(g * t_s + i, query(i, g), kn_ref[...], vn_ref[...], tnew_ref[g, i])
            return carry

        lax.fori_loop(0, t_s, fresh, 0, unroll=2)

    @pl.when(c == 1)
    def _():
        def mid(i, carry):
            rho, hi = i % 4, i // 4
            kts, vts, bs = [], [], []
            for rr in range(4):
                k_src, v_src = (kl_ref, vl_ref) if rr < 2 else (kb_ref, vb_ref)
                r_idx = rho + 4 * (rr % 2)
                kts.append(k_src[:, r_idx])
                vts.append(v_src[:, r_idx])
                bs.append(tm_ref[hi - rr + 3])
            absorb(t_s + i, query(i, 1), jnp.concatenate(kts, axis=0), jnp.concatenate(vts, axis=0),
                   jnp.concatenate(bs, axis=0))
            return carry

        lax.fori_loop(0, t_s, mid, 0, unroll=2)

    @pl.when(c == n_steps - 1)
    def _():
        near_groups = SWA_BLK // SWA_RES
        n_near = near_groups * (SWA_RES // 2)

        def near(i, carry):
            kts, vts, bs = [], [], []
            for k_src, v_src, off in ((kl_ref, vl_ref, 7), (kb_ref, vb_ref, 7 + SWA_RES // 2)):
                kts.append(k_src[mb - near_groups:mb].reshape(n_near, SWA_HEADS, HEAD_DIM))
                vts.append(v_src[mb - near_groups:mb].reshape(n_near, SWA_HEADS, HEAD_DIM))
                bs += [tnr_ref[pl.ds(SWA_RES * grp + off - i, SWA_RES // 2)] for grp in range(near_groups)]
            absorb(i, query(i, 0), jnp.concatenate(kts, axis=0), jnp.concatenate(vts, axis=0),
                   jnp.concatenate(bs, axis=0))
            return carry

        lax.fori_loop(0, t_s, near, 0, unroll=2)

        for i in range(t_s):
            lse = [m_s[g * t_s + i] + jnp.log2(l_s[g * t_s + i]) for g in range(N_DIL)]
            mx = jnp.maximum(jnp.maximum(lse[0], lse[1]), lse[2])
            e = [jnp.exp2(x - mx) for x in lse]
            num = sum(e[g] * (acc[g * t_s + i] / l_s[g * t_s + i]) for g in range(N_DIL))
            o_ref[i] = num / (e[0] + e[1] + e[2])


def _swa_sample(q_new, k_new, v_new, cache_k, cache_v, tables, n_b, t_s):
    l_cache = cache_k.shape[1]
    n_groups = l_cache // SWA_RES
    n_steps = n_groups // SWA_MB - 1
    assert t_s == SWA_RES // 2 and l_cache % (SWA_RES * SWA_MB) == 0 and l_cache >= SWA_MAX_WINDOW and n_steps >= 3
    ck = cache_k.reshape(n_b, n_groups, SWA_RES, SWA_HEADS, HEAD_DIM)
    cv = cache_v.reshape(n_b, n_groups, SWA_RES, SWA_HEADS, HEAD_DIM)
    half = (SWA_MB, SWA_RES // 2, SWA_HEADS, HEAD_DIM)
    a_spec = pl.BlockSpec((None,) + half, lambda b, c: (b, c, 0, 0, 0))
    last_spec = lambda r: pl.BlockSpec((None,) + half, lambda b, c: (b, n_steps, r, 0, 0))
    new_spec = pl.BlockSpec((None, t_s, SWA_HEADS, HEAD_DIM), lambda b, c: (b, 0, 0, 0))
    const = lambda t: pl.BlockSpec(t.shape, lambda b, c: (0,) * t.ndim)
    slots = N_DIL * t_s
    state = pltpu.VMEM((slots, SWA_HEADS, HEAD_DIM), F32)
    return pl.pallas_call(
        functools.partial(_swa_sample_kernel, t_s=t_s, n_groups=n_groups),
        grid=(n_b, n_steps),
        in_specs=[pl.BlockSpec((None, t_s, N_DIL, SWA_HEADS, HEAD_DIM), lambda b, c: (b, 0, 0, 0, 0)),
                  new_spec, new_spec, a_spec, a_spec, last_spec(0), last_spec(0), last_spec(1), last_spec(1)]
                 + [const(t) for t in tables],
        out_specs=new_spec,
        out_shape=jax.ShapeDtypeStruct((n_b, t_s, SWA_HEADS, HEAD_DIM), F32),
        scratch_shapes=[state, state, state],
        compiler_params=_params("arbitrary", "arbitrary"),
        name="swa_sample",
    )(q_new, k_new, v_new, ck, cv, ck, cv, ck, cv, *tables)


def _swa_mixer(x, n_p, n_b, t_s, cache_k, cache_v, w_in, w_out, rel_bias):
    tm = _pick_tile(math.gcd(n_p, n_b * t_s), PROJ_TM)
    proj = _matmul(x, w_in, SWA_NQ + 2 * SWA_NKV, tm, 512)
    biases = _group_biases(rel_bias)
    o_p = _swa_prompt(proj, _prompt_bias_tables(biases), n_p)
    k_all = proj[:, SWA_NQ:SWA_NQ + SWA_NKV]
    v_all = proj[:, SWA_NQ + SWA_NKV:]
    per_head = (n_b, t_s, SWA_HEADS, HEAD_DIM)
    o_s = _swa_sample(proj[n_p:, :SWA_NQ].reshape(n_b, t_s, N_DIL, SWA_HEADS, HEAD_DIM),
                      k_all[n_p:].reshape(per_head), v_all[n_p:].reshape(per_head), cache_k, cache_v,
                      _sample_bias_tables(biases, cache_k.shape[1], t_s), n_b, t_s)
    y = _matmul((o_p, o_s.reshape(n_b * t_s, SWA_NKV)), w_out, D_MODEL, tm, 512)
    return y, k_all, v_all


def _pad_rows(w, rows):
    return jnp.zeros((rows, w.shape[1]), w.dtype).at[:w.shape[0]].set(w)


def _pad_cols(w, cols):
    return jnp.zeros((w.shape[0], cols), w.dtype).at[:, :w.shape[1]].set(w)


def _gla_mixer(x0, n_p, n_b, t_s, state0, w_in, w_gate, b_gate, norm_w, w_out):
    tm = _pick_tile(math.gcd(n_p, n_b * t_s), PROJ_TM)
    proj = _matmul(x0, w_in, GLA_MAIN, tm, 512)
    glow = _matmul(x0, _pad_cols(w_in[:, GLA_MAIN:], LANES), LANES, tm, LANES)
    wg_pad = _pad_rows(w_gate, LANES)
    bg = b_gate.reshape(1, GLA_DK)
    nw = norm_w.reshape(1, GLA_DV_HEAD)
    zero_state = jnp.zeros((1, GLA_HEADS, GLA_DK_HEAD, GLA_DV_HEAD), F32)
    o_p, st_p = _gla(proj, glow, wg_pad, bg, nw, zero_state, n_seq=1, seq_len=n_p, row_start=0,
                     rows_per_step=512, chunk=GLA_CHUNK, mxu_intra=True)
    o_s, st_s = _gla(proj, glow, wg_pad, bg, nw, state0, n_seq=n_b, seq_len=t_s, row_start=n_p,
                     rows_per_step=t_s, chunk=math.gcd(t_s, GLA_CHUNK), mxu_intra=False)
    return _matmul((o_p, o_s), w_out, D_MODEL, tm, 512), st_p, st_s


def kernel(x_prompt, x_sample, state_gla, cache_swa_k, cache_swa_v, gla_w_in, gla_w_gate, gla_b_gate, gla_norm_w,
           gla_w_out, swa_w_in, swa_w_out, rel_bias, w_router, b_router, moe_w_gate, moe_w_up, moe_w_down,
           ln1_g, ln1_b, ln2_g, ln2_b):
    n_p = x_prompt.shape[0] * x_prompt.shape[1]
    n_b, t_s = x_sample.shape[0], x_sample.shape[1]
    n_s = n_b * t_s
    assert x_prompt.shape[0] == 1, "one prompt sequence"
    x = (x_prompt.reshape(n_p, D_MODEL), x_sample.reshape(n_s, D_MODEL))
    wr_pad = _pad_cols(w_router, LANES)
    br_pad = _pad_cols(b_router.reshape(1, N_EXPERTS), LANES)
    tm_ln = _pick_tile(math.gcd(n_p, n_s), LN_TM)

    gla_p, gla_s, kp, vp, ksm, vsm = [], [], [], [], [], []
    for i in range(DEPTH):
        j = i // 2
        if i % 2 == 0:
            y, st_p, st_s = _gla_mixer(x, n_p, n_b, t_s, state_gla[j], gla_w_in[j], gla_w_gate[j], gla_b_gate[j],
                                       gla_norm_w[j], gla_w_out[j])
            gla_p.append(st_p)
            gla_s.append(st_s)
        else:
            y, k_all, v_all = _swa_mixer(x, n_p, n_b, t_s, cache_swa_k[j], cache_swa_v[j], swa_w_in[j], swa_w_out[j],
                                         rel_bias)
            keep = min(SWA_MAX_WINDOW, n_p)
            kp.append(k_all[n_p - keep:n_p].reshape(1, keep, SWA_HEADS, HEAD_DIM))
            vp.append(v_all[n_p - keep:n_p].reshape(1, keep, SWA_HEADS, HEAD_DIM))
            ksm.append(k_all[n_p:].reshape(n_b, t_s, SWA_HEADS, HEAD_DIM))
            vsm.append(v_all[n_p:].reshape(n_b, t_s, SWA_HEADS, HEAD_DIM))
        x1, idx, gate, counts = _ln_route(x, y, ln1_g[i:i + 1], ln1_b[i:i + 1], wr_pad, br_pad, tm_ln)
        last = i == DEPTH - 1
        x = _moe_ln(x1, idx, gate, counts, moe_w_gate, moe_w_up, moe_w_down, i, ln2_g[i:i + 1], ln2_b[i:i + 1],
                    n_split=n_p if last else None)
    y_p, y_s = x
    return (y_p.reshape(x_prompt.shape), y_s.reshape(x_sample.shape), jnp.stack(gla_p), jnp.stack(gla_s),
            jnp.stack(kp), jnp.stack(vp), jnp.stack(ksm), jnp.stack(vsm))
```

```python
import functools
import math

import numpy as np
import jax
import jax.numpy as jnp
from jax import lax
from jax.experimental import pallas as pl
from jax.experimental.pallas import tpu as pltpu

F32 = jnp.float32
BF16 = jnp.bfloat16
I32 = jnp.int32

D_MODEL = 2048
DEPTH = 2
GLA_HEADS = 4
GLA_DK_HEAD = 256
GLA_DV_HEAD = 512
GLA_DK = GLA_HEADS * GLA_DK_HEAD
GLA_DV = GLA_HEADS * GLA_DV_HEAD
GLA_GATE_RANK = 16
GLA_GATE_TAU = 16.0
GLA_CHUNK = 64
GLA_SUB = 16
GLA_SEQS_PER_STEP = 8
GLA_MAIN = 2 * GLA_DK + 2 * GLA_DV
RMS_EPS = 1e-6
HEAD_DIM = 128
SWA_HEADS = 16
DIL_GROUPS = ((128, 1), (512, 4), (2048, 16))
N_DIL = len(DIL_GROUPS)
SWA_NQ = N_DIL * SWA_HEADS * HEAD_DIM
SWA_NKV = SWA_HEADS * HEAD_DIM
SWA_MAX_WINDOW = 2048
NUM_BUCKETS = 32
MAX_DISTANCE = 2048
N_EXPERTS = 16
N_EXPERT_GROUPS = 4
EXPERTS_PER_GROUP = 4
D_EXPERT = 1024
DEEPNORM_ALPHA = (2 * DEPTH) ** 0.25
LN_EPS = 1e-5

LANES = 128
SUBLANES = 8
VMEM_LIMIT_BYTES = 56 * 1024 * 1024
NEG_BIG = -1e30


PROJ_TM = 1024
LN_TM = 256


def _pick_tile(n, preferred):
    t = preferred
    while t > SUBLANES and n % t:
        t //= 2
    assert n % t == 0, (n, t)
    return t


def _params(*sem):
    return pltpu.CompilerParams(dimension_semantics=sem, vmem_limit_bytes=VMEM_LIMIT_BYTES)


def _dot(a, b):
    return jnp.dot(a, b, preferred_element_type=F32)


def _dot_nt(a, b):
    return lax.dot_general(a, b, (((1,), (1,)), ((), ())), preferred_element_type=F32)


def _dot_tn(a, b):
    return lax.dot_general(a, b, (((0,), (0,)), ((), ())), preferred_element_type=F32)


def _split3(a):
    hi = a.astype(BF16)
    r1 = a - hi.astype(F32)
    mid = r1.astype(BF16)
    lo = (r1 - mid.astype(F32)).astype(BF16)
    return hi, mid, lo


def _row_segments(x):
    segs = tuple(x) if isinstance(x, (tuple, list)) else (x,)
    assert len(segs) in (1, 2)
    return segs


def _segment_specs(segs, tm):
    width = segs[0].shape[1]
    assert all(s.shape[0] % tm == 0 and s.shape[1] == width for s in segs)
    first = segs[0].shape[0] // tm
    if len(segs) == 1:
        return [pl.BlockSpec((tm, width), lambda i, *_: (i, 0))], first
    return [pl.BlockSpec((tm, width), lambda i, *_: (jnp.minimum(i, first - 1), 0)),
            pl.BlockSpec((tm, width), lambda i, *_: (jnp.maximum(i - first, 0), 0))], first


def _segment_tile(x_refs, first_tiles):
    if len(x_refs) == 1:
        return x_refs[0][...]
    return jnp.where(pl.program_id(0) < first_tiles, x_refs[0][...], x_refs[1][...])


def _mm_kernel(*refs, first_tiles):
    *x_refs, w_ref, o_ref, xb_ref = refs

    @pl.when(pl.program_id(1) == 0)
    def _():
        xb_ref[...] = _segment_tile(x_refs, first_tiles).astype(BF16)

    o_ref[...] = _dot(xb_ref[...], w_ref[...].astype(BF16)).astype(o_ref.dtype)


def _matmul(x, w, n_out, tm, tn):
    segs = _row_segments(x)
    m = sum(s.shape[0] for s in segs)
    k = segs[0].shape[1]
    assert n_out % tn == 0 and w.shape[0] == k
    x_specs, first = _segment_specs(segs, tm)
    return pl.pallas_call(
        functools.partial(_mm_kernel, first_tiles=first),
        grid=(m // tm, n_out // tn),
        in_specs=x_specs + [pl.BlockSpec((k, tn), lambda i, j: (0, j))],
        out_specs=pl.BlockSpec((tm, tn), lambda i, j: (i, j)),
        out_shape=jax.ShapeDtypeStruct((m, n_out), F32),
        scratch_shapes=[pltpu.VMEM((tm, k), BF16)],
        compiler_params=_params("arbitrary", "arbitrary"),
        name="dense_proj",
    )(*segs, w)


def _gla_kernel(q_ref, k_ref, v_ref, r_ref, gl_ref, wg_ref, bg_ref, nw_ref, s0_ref,
                o_ref, sf_ref, state, bsc, *, chunk, n_chunks, seqs, mxu_intra):
    c_sz = chunk
    r_sz = chunk * n_chunks
    t = pl.program_id(2)

    @pl.when(t == 0)
    def _():
        state[...] = s0_ref[...]

    rowi = lax.broadcasted_iota(I32, (c_sz, GLA_DK_HEAD), 0)
    tri = (lax.broadcasted_iota(I32, (c_sz, c_sz), 0) >= lax.broadcasted_iota(I32, (c_sz, c_sz), 1)).astype(BF16)
    ones = jnp.ones((c_sz, LANES), BF16)
    wg =wg_ref[...].astype(BF16)

    def one_chunk(sb, c):
        rows = pl.ds(pl.multiple_of(sb * r_sz + c * c_sz, c_sz), c_sz)
        q = q_ref[rows, :] * (GLA_DK_HEAD ** -0.5)
        k = k_ref[rows, :]
        v = v_ref[rows, :]
        z = _dot(gl_ref[rows, :].astype(BF16), wg) + bg_ref[...]
        la = (jnp.minimum(z, 0.0) - jnp.log(1.0 + jnp.exp(-jnp.abs(z)))) * (1.0 / GLA_GATE_TAU)
        l_hi, l_mid, l_lo = _split3(la)
        b = _dot(tri, l_hi) + (_dot(tri, l_mid) + _dot(tri, l_lo))
        dcol = _dot_tn(l_hi, ones) + (_dot_tn(l_mid, ones) + _dot_tn(l_lo, ones))
        b_last = b[c_sz - 1:c_sz, :]

        s_old = state[sb]
        o = _dot((q * jnp.exp(b)).astype(BF16), s_old.astype(BF16))

        if mxu_intra:
            n_sub = c_sz // GLA_SUB
            parts = [jnp.zeros((GLA_SUB, c_sz), F32)]
            colc = lax.broadcasted_iota(I32, (GLA_SUB, c_sz), 1)
            for blk in range(1, n_sub):
                lo = blk * GLA_SUB
                ref_b = b[lo - 1:lo, :]
                q_rel = q[lo:lo + GLA_SUB, :] * jnp.exp(b[lo:lo + GLA_SUB, :] - ref_b)
                k_rel = k * jnp.exp(jnp.minimum(ref_b - b, 0.0))
                s_blk = _dot_nt(q_rel.astype(BF16), k_rel.astype(BF16))
                parts.append(jnp.where(colc < lo, s_blk, 0.0))
            att = jnp.concatenate(parts, axis=0)

            b3 = b.reshape(n_sub, GLA_SUB, GLA_DK_HEAD)
            q3 = q.reshape(n_sub, GLA_SUB, GLA_DK_HEAD)
            k3 = k.reshape(n_sub, GLA_SUB, GLA_DK_HEAD)
            row3 = lax.broadcasted_iota(I32, b3.shape, 1)
            blk_lane0 = (lax.broadcasted_iota(I32, (c_sz, c_sz), 0) // GLA_SUB) * GLA_SUB
            lane_c = lax.broadcasted_iota(I32, (c_sz, c_sz), 1)
            for jj in range(GLA_SUB):
                e = jnp.exp(jnp.where(row3 >= jj, b3 - b3[:, jj:jj + 1, :], -jnp.inf))
                col = jnp.sum(q3 * e * k3[:, jj:jj + 1, :], axis=-1, keepdims=True).reshape(c_sz, 1)
                att = jnp.where(lane_c == blk_lane0 + jj, col, att)
            o = o + _dot(att.astype(BF16), v.astype(BF16))
        else:
            bsc[sb] = b
            qk_base = sb * r_sz + c * c_sz

            def jbody(j, acc):
                kj = k_ref[pl.ds(qk_base + j, 1), :]
                vj = v_ref[pl.ds(qk_base + j, 1), :]
                e = jnp.exp(jnp.where(rowi >= j, b - bsc[sb, pl.ds(j, 1), :], -jnp.inf))
                return acc + jnp.sum(q * e * kj, axis=-1, keepdims=True) * vj

            o = lax.fori_loop(0, c_sz, jbody, o, unroll=True)

        kt = (k * jnp.exp(b_last - b)).astype(BF16)
        decay = jnp.exp(dcol)
        ds = _dot_tn(kt, v.astype(BF16))
        state[sb] = s_old * jnp.tile(decay, (1, GLA_DV_HEAD // LANES)) + ds

        o = o * lax.rsqrt(jnp.mean(o * o, axis=-1, keepdims=True) + RMS_EPS) * nw_ref[...]
        r = r_ref[rows, :]
        o_ref[rows, :] = o * (r / (1.0 + jnp.exp(-r)))

    for sb in range(seqs):
        if n_chunks == 1:
            one_chunk(sb, 0)
        else:
            def body(c, carry, sb=sb):
                one_chunk(sb, c)
                return carry
            lax.fori_loop(0, n_chunks, body, 0, unroll=2)

    @pl.when(t == pl.num_programs(2) - 1)
    def _():
        sf_ref[...] = state[...]


def _gla(proj, glow, w_gate_pad, b_gate, norm_w, s0, *, n_seq, seq_len, row_start, rows_per_step, chunk,
         mxu_intra):
    r_sz = rows_per_step
    steps = seq_len // r_sz
    seqs = math.gcd(n_seq, GLA_SEQS_PER_STEP) if steps == 1 else 1
    blk_rows = seqs * r_sz
    assert seq_len % r_sz == 0 and r_sz % chunk == 0 and row_start % blk_rows == 0
    rb0 = row_start // blk_rows
    kq, kv = GLA_DK // GLA_DK_HEAD, GLA_DK // GLA_DV_HEAD

    def rowblk(b, t):
        return rb0 + b * steps + t

    st_spec = pl.BlockSpec((seqs, None, GLA_DK_HEAD, GLA_DV_HEAD), lambda b, h, t: (b, h, 0, 0))
    kern = functools.partial(_gla_kernel, chunk=chunk, n_chunks=r_sz // chunk, seqs=seqs, mxu_intra=mxu_intra)
    return pl.pallas_call(
        kern,
        grid=(n_seq // seqs, GLA_HEADS, steps),
        in_specs=[
            pl.BlockSpec((blk_rows, GLA_DK_HEAD), lambda b, h, t: (rowblk(b, t), h)),
            pl.BlockSpec((blk_rows, GLA_DK_HEAD), lambda b, h, t: (rowblk(b, t), kq + h)),
            pl.BlockSpec((blk_rows, GLA_DV_HEAD), lambda b, h, t: (rowblk(b, t), 2 * kv + h)),
            pl.BlockSpec((blk_rows, GLA_DV_HEAD), lambda b, h, t: (rowblk(b, t), 2 * kv + GLA_HEADS + h)),
            pl.BlockSpec((blk_rows, LANES), lambda b, h, t: (rowblk(b, t), 0)),
            pl.BlockSpec((LANES, GLA_DK_HEAD), lambda b, h, t: (0, h)),
            pl.BlockSpec((1, GLA_DK_HEAD), lambda b, h, t: (0, h)),
            pl.BlockSpec((1, GLA_DV_HEAD), lambda b, h, t: (0, 0)),
            st_spec,
        ],
        out_specs=[
            pl.BlockSpec((blk_rows, GLA_DV_HEAD), lambda b, h, t: (b * steps + t, h)),
            st_spec,
        ],
        out_shape=[jax.ShapeDtypeStruct((n_seq * seq_len, GLA_DV), F32),
                   jax.ShapeDtypeStruct((n_seq, GLA_HEADS, GLA_DK_HEAD, GLA_DV_HEAD), F32)],
        scratch_shapes=[pltpu.VMEM((seqs, GLA_DK_HEAD, GLA_DV_HEAD), F32),
                        pltpu.VMEM((seqs, chunk, GLA_DK_HEAD), F32)],
        compiler_params=_params("arbitrary", "arbitrary", "arbitrary"),
        name="gla_chunked",
    )(proj, proj, proj, proj, glow, w_gate_pad, b_gate, norm_w, s0)


def _layer_norm(x, g, b):
    mu = jnp.mean(x, axis=-1, keepdims=True)
    xc = x - mu
    var = jnp.mean(xc * xc, axis=-1, keepdims=True)
    return xc * lax.rsqrt(var + LN_EPS) * g + b


def _route(x, wr_ref, br_ref, idx_ref, gate_ref, cnt_ref, seen):
    tm = x.shape[0]
    xh, xm, xl = _split3(x)
    wh, wm, wl = _split3(wr_ref[...])
    logits = _dot_nt(wh, xh) + ((_dot_nt(wm, xh) + _dot_nt(wh, xm))
                                + ((_dot_nt(wm, xm) + _dot_nt(wl, xh)) + _dot_nt(wh, xl)))
    scores = 1.0 / (1.0 + jnp.exp(-logits))
    lane = lax.broadcasted_iota(I32, (N_EXPERTS, tm), 0).astype(F32)
    sel = scores + br_ref[...]
    big = float(N_EXPERTS)
    best = None
    for g in range(N_EXPERT_GROUPS):
        in_g = (lane >= g * EXPERTS_PER_GROUP) & (lane < (g + 1) * EXPERTS_PER_GROUP)
        vg = jnp.where(in_g, sel, -jnp.inf)
        m1 = jnp.max(vg, axis=0, keepdims=True)
        i1 = jnp.min(jnp.where(vg == m1, lane, big), axis=0, keepdims=True)
        vg2 = jnp.where(lane == i1, -jnp.inf, vg)
        m2 = jnp.max(vg2, axis=0, keepdims=True)
        i2 = jnp.min(jnp.where(vg2 == m2, lane, big), axis=0, keepdims=True)
        gs = m1 + m2
        if best is None:
            best, b1, b2 = gs, i1, i2
        else:
            better = gs > best
            best = jnp.where(better, gs, best)
            b1 = jnp.where(better, i1, b1)
            b2 = jnp.where(better, i2, b2)
    w1 = jnp.sum(jnp.where(lane == b1, scores, 0.0), axis=0, keepdims=True)
    w2 = jnp.sum(jnp.where(lane == b2, scores, 0.0), axis=0, keepdims=True)
    den = w1 + w2
    picked = jnp.logical_or(lane == b1, lane == b2).astype(F32)
    earlier = (lax.broadcasted_iota(I32, (tm, tm), 0) < lax.broadcasted_iota(I32, (tm, tm), 1)).astype(BF16)
    before = _dot(picked.astype(BF16), earlier) + seen[...]
    r1 = jnp.sum(jnp.where(lane == b1, before, 0.0), axis=0, keepdims=True)
    r2 = jnp.sum(jnp.where(lane == b2, before, 0.0), axis=0, keepdims=True)
    seen[...] = seen[...] + jnp.sum(picked, axis=1, keepdims=True)
    cnt_ref[...] = jnp.broadcast_to(seen[...], cnt_ref.shape)
    row = lax.broadcasted_iota(I32, (SUBLANES, tm), 0)
    idx_ref[...] = jnp.where(row == 0, b1, jnp.where(row == 1, b2, jnp.where(row == 2, r1, jnp.where(
        row == 3, r2, 0.0)))).astype(I32)
    gate_ref[...] = jnp.where(row == 0, w1 / den, jnp.where(row == 1, w2 / den, 0.0))


def _ln_route_kernel(*refs, first_tiles):
    *x_refs, y_ref, g_ref, b_ref, wr_ref, br_ref, o_ref, idx_ref, gate_ref, cnt_ref, seen = refs

    @pl.when(pl.program_id(0) == 0)
    def _():
        seen[...] = jnp.zeros(seen.shape, F32)

    x = _segment_tile(x_refs, first_tiles)
    x1 = _layer_norm(DEEPNORM_ALPHA * x + y_ref[...], g_ref[...], b_ref[...])
    o_ref[...] = x1
    _route(x1, wr_ref, br_ref, idx_ref, gate_ref, cnt_ref, seen)


def _ln_route(x, y, g, b, w_router_t, b_router_col, tm):
    segs = _row_segments(x)
    n, d = y.shape
    row = lambda i: (i, 0)
    col = lambda i: (0, i)
    fixed = lambda i: (0, 0)
    x_specs, first = _segment_specs(segs, tm)
    return pl.pallas_call(
        functools.partial(_ln_route_kernel, first_tiles=first),
        grid=(n // tm,),
        in_specs=x_specs + [pl.BlockSpec((tm, d), row),
                  pl.BlockSpec((1, d), fixed), pl.BlockSpec((1, d), fixed),
                  pl.BlockSpec((N_EXPERTS, d), fixed), pl.BlockSpec((N_EXPERTS, 1), fixed)],
        out_specs=[pl.BlockSpec((tm, d), row), pl.BlockSpec((SUBLANES, tm), col), pl.BlockSpec((SUBLANES, tm), col),
                   pl.BlockSpec((N_EXPERTS, LANES), fixed)],
        out_shape=[jax.ShapeDtypeStruct((n, d), F32), jax.ShapeDtypeStruct((SUBLANES, n), I32),
                   jax.ShapeDtypeStruct((SUBLANES, n), F32), jax.ShapeDtypeStruct((N_EXPERTS, LANES), F32)],
        scratch_shapes=[pltpu.VMEM((N_EXPERTS, 1), F32)],
        compiler_params=_params("arbitrary"),
        name="deepnorm_ln_router",
    )(*segs, y, g, b, w_router_t, b_router_col)


def _gmm1_kernel(src_ref, te_ref, ch_ref, nu_ref, x_hbm, wg_ref, wu_ref, h_ref, xbuf, wgb, wub, sems, *, tm):
    i = pl.program_id(0)
    n_used = nu_ref[0]

    def row_copy(tile, r, slot):
        return pltpu.make_async_copy(x_hbm.at[pl.ds(src_ref[tile * tm + r], 1), :],
                                     xbuf.at[slot, pl.ds(r, 1), :], sems.at[slot])

    def fetch(tile, slot):
        def issue(r, c):
            row_copy(tile, r, slot).start()
            return c
        lax.fori_loop(0, tm, issue, 0, unroll=DMA_UNROLL)

    def drain(tile, slot):
        def wait_row(r, c):
            row_copy(tile, r, slot).wait()
            return c
        lax.fori_loop(0, tm, wait_row, 0, unroll=DMA_UNROLL)

    @pl.when(i == 0)
    def _():
        fetch(0, 0)

    @pl.when(i < n_used)
    def _():
        slot = i % 2
        drain(i, slot)

        @pl.when(ch_ref[i] == 1)
        def _():
            wgb[...] = wg_ref[...].astype(BF16)
            wub[...] = wu_ref[...].astype(BF16)

        x = xbuf[slot].astype(BF16)
        a = _dot(x, wgb[...])
        for r in range(tm):
            row_copy(i + 1, r, 1 - slot).start()
        u = _dot(x, wub[...])
        h_ref[...] = ((a / (1.0 + jnp.exp(-a))) * u).astype(BF16)

    @pl.when(i >= n_used)
    def _():
        @pl.when(i == n_used)
        def _():
            drain(i, i % 2)

        h_ref[...] = jnp.zeros(h_ref.shape, h_ref.dtype)


def _gmm1(x, src_row, w_gate, w_up, layer, meta, tm):
    te, _, ch, nu = meta
    n_tiles = src_row.shape[0] // tm
    d = x.shape[1]
    f = w_gate.shape[-1]
    wspec = pl.BlockSpec((None, None, d, f), lambda i, src, te, ch, nu: (layer, te[i], 0, 0))
    return pl.pallas_call(
        functools.partial(_gmm1_kernel, tm=tm),
        grid_spec=pltpu.PrefetchScalarGridSpec(
            num_scalar_prefetch=4,
            grid=(n_tiles,),
            in_specs=[pl.BlockSpec(memory_space=pl.ANY), wspec, wspec],
            out_specs=pl.BlockSpec((tm, f), lambda i, src, te, ch, nu: (i, 0)),
            scratch_shapes=[pltpu.VMEM((2, tm, d), F32), pltpu.VMEM((d, f), BF16), pltpu.VMEM((d, f), BF16),
                            pltpu.SemaphoreType.DMA((2,))],
        ),
        out_shape=jax.ShapeDtypeStruct((n_tiles * tm, f), BF16),
        compiler_params=_params("arbitrary"),
        name="moe_gather_gate_up",
    )(src_row, te, ch, nu, x, w_gate, w_up)


def _gmm2_kernel(te_ref, tb_ref, ch_ref, nu_ref, h_ref, wd_ref, y_ref, wdb):
    i = pl.program_id(1)

    @pl.when(i < nu_ref[0])
    def _():
        @pl.when(ch_ref[i] == 1)
        def _():
            wdb[...] = wd_ref[...].astype(BF16)

        y_ref[...] = _dot(h_ref[...], wdb[...])

    @pl.when(i >= nu_ref[0])
    def _():
        y_ref[...] = jnp.zeros(y_ref.shape, y_ref.dtype)


def _gmm2(h, w_down, layer, meta, tm, tn):
    te, tb, ch, nu = meta
    n_tiles = h.shape[0] // tm
    f = h.shape[1]
    d = w_down.shape[-1]
    return pl.pallas_call(
        _gmm2_kernel,
        grid_spec=pltpu.PrefetchScalarGridSpec(
            num_scalar_prefetch=4,
            grid=(d // tn, n_tiles),
            in_specs=[pl.BlockSpec((tm, f), lambda j, i, te, tb, ch, nu: (tb[i], 0)),
                      pl.BlockSpec((None, None, f, tn), lambda j, i, te, tb, ch, nu: (layer, te[i], 0, j))],
            out_specs=pl.BlockSpec((tm, tn), lambda j, i, te, tb, ch, nu: (i, j)),
            scratch_shapes=[pltpu.VMEM((f, tn), BF16)],
        ),
        out_shape=jax.ShapeDtypeStruct((n_tiles * tm, d), F32),
        compiler_params=_params("arbitrary", "arbitrary"),
        name="moe_down",
    )(te, tb, ch, nu, h, w_down)


def _combine_ln_kernel(pos_ref, y_hbm, x_ref, gate_ref, g_ref, b_ref, *rest, tm, n_split_tiles):
    if n_split_tiles is None:
        (o_ref, ybuf, sems) = rest
    else:
        (op_ref, os_ref, ybuf, sems) = rest
    i = pl.program_id(0)

    def row_copy(tile, r, k, slot):
        return pltpu.make_async_copy(y_hbm.at[pl.ds(pos_ref[2 * (tile * tm + r) + k], 1), :],
                                     ybuf.at[slot, k, pl.ds(r, 1), :], sems.at[slot])

    def fetch(tile, slot):
        def issue(r, c):
            row_copy(tile, r, 0, slot).start()
            row_copy(tile, r, 1, slot).start()
            return c
        lax.fori_loop(0, tm, issue, 0, unroll=DMA_UNROLL)

    @pl.when(i == 0)
    def _():
        fetch(0, 0)

    @pl.when(i + 1 < pl.num_programs(0))
    def _():
        fetch(i + 1, (i + 1) % 2)

    slot = i % 2

    def drain(r, c):
        row_copy(i, r, 0, slot).wait()
        row_copy(i, r, 1, slot).wait()
        return c

    lax.fori_loop(0, tm, drain, 0, unroll=DMA_UNROLL)
    gate = gate_ref[...]
    moe = gate[:, 0:1] * ybuf[slot, 0] + gate[:, 1:2] * ybuf[slot, 1]
    out = _layer_norm(DEEPNORM_ALPHA * x_ref[...] + moe, g_ref[...], b_ref[...])
    if n_split_tiles is None:
        o_ref[...] = out
    else:
        @pl.when(i < n_split_tiles)
        def _():
            op_ref[...] = out

        @pl.when(i >= n_split_tiles)
        def _():
            os_ref[...] = out


def _combine_ln(y_sorted, pos, x, gate, g, b, tm, n_split=None):
    n, d = x.shape
    row = lambda i, pos: (i, 0)
    fixed = lambda i, pos: (0, 0)
    if n_split is None:
        nst = None
        out_specs = pl.BlockSpec((tm, d), row)
        out_shape = jax.ShapeDtypeStruct((n, d), F32)
    else:
        nst = n_split // tm
        out_specs = [pl.BlockSpec((tm, d), lambda i, pos: (jnp.minimum(i, nst - 1), 0)),
                     pl.BlockSpec((tm, d), lambda i, pos: (jnp.maximum(i - nst, 0), 0))]
        out_shape = [jax.ShapeDtypeStruct((n_split, d), F32), jax.ShapeDtypeStruct((n - n_split, d), F32)]
    return pl.pallas_call(
        functools.partial(_combine_ln_kernel, tm=tm, n_split_tiles=nst),
        grid_spec=pltpu.PrefetchScalarGridSpec(
            num_scalar_prefetch=1,
            grid=(n // tm,),
            in_specs=[pl.BlockSpec(memory_space=pl.ANY), pl.BlockSpec((tm, d), row),
                      pl.BlockSpec((tm, 2), row), pl.BlockSpec((1, d), fixed), pl.BlockSpec((1, d), fixed)],
            out_specs=out_specs,
            scratch_shapes=[pltpu.VMEM((2, 2, tm, d), F32), pltpu.SemaphoreType.DMA((2,))],
        ),
        out_shape=out_shape,
        compiler_params=_params("arbitrary"),
        name="moe_combine_ln",
    )(pos, y_sorted, x, gate, g, b)


def _dispatch_meta(idx, counts, tm, n_tiles):
    n = idx.shape[1]
    e_flat = idx[0:2].T.reshape(-1)
    rank = idx[2:4].T.reshape(-1)
    counts = counts[:, 0].astype(I32)
    ptiles = (counts + tm - 1) // tm
    tile_end = jnp.cumsum(ptiles)
    n_used = tile_end[-1]
    row_off = (tile_end - ptiles) * tm
    pos = row_off[e_flat] + rank
    tok = jnp.arange(2 * n, dtype=I32) // 2
    src_row = jnp.zeros((n_tiles * tm,), I32).at[pos].set(tok)
    tile_id = jnp.minimum(jnp.arange(n_tiles, dtype=I32), n_used - 1)
    tile_expert = jnp.sum((tile_id[:, None] >= tile_end[None, :]).astype(I32), axis=1)
    changed = jnp.concatenate([jnp.ones((1,), I32), (tile_expert[1:] != tile_expert[:-1]).astype(I32)])
    nu = n_used.reshape(1).astype(I32)
    return src_row, pos.astype(I32), (tile_expert, tile_id.astype(I32), changed, nu)


DMA_UNROLL = 8
MOE_TM = 256
MOE_TN = 2048


def _moe_ln(x1, idx, gate, counts, w_gate, w_up, w_down, layer, g, b, n_split=None):
    n = x1.shape[0]
    n_tiles = -(-2 * n // MOE_TM) + N_EXPERTS + 1
    src_row, pos, meta = _dispatch_meta(idx, counts, MOE_TM, n_tiles)
    h = _gmm1(x1, src_row, w_gate, w_up, layer, meta, MOE_TM)
    y = _gmm2(h, w_down, layer, meta, MOE_TM, MOE_TN)
    tm = _pick_tile(n if n_split is None else math.gcd(n, n_split), LN_TM)
    return _combine_ln(y, pos, x1, gate[0:2].T, g, b, tm, n_split)


SWA_BLK = 128
SWA_SUPER = SWA_BLK * 16
SWA_SCALE = HEAD_DIM ** -0.5
SWA_TOEPLITZ_PERIOD = 4 * SWA_BLK


def _t5_bucket(dist):
    max_exact = NUM_BUCKETS // 2
    d = jnp.maximum(dist.astype(F32), 1.0)
    large = max_exact + (jnp.log(d / max_exact) / math.log(MAX_DISTANCE / max_exact)
                         * (NUM_BUCKETS - max_exact)).astype(I32)
    large = jnp.minimum(large, NUM_BUCKETS - 1)
    return jnp.where(dist < max_exact, dist, large)


def _group_biases(rel_bias):
    out = []
    for g, (w, d) in enumerate(DIL_GROUPS):
        dist = d * jnp.arange(w // d + 1, dtype=I32)
        b = rel_bias[_t5_bucket(dist)]
        out.append(b[:, g * SWA_HEADS:(g + 1) * SWA_HEADS].T.astype(F32))
    return out


def _prompt_bias_tables(biases):
    b_all = jnp.concatenate(biases, axis=0)
    mask = jnp.full((b_all.shape[0], SWA_TOEPLITZ_PERIOD - SWA_BLK - 1), NEG_BIG, F32)
    return jnp.concatenate([b_all[:, ::-1] * LOG2_E, mask], axis=1)[:, None, :]


def _swa_prompt_kernel(q0_ref, q1_ref, q2_ref, kc_ref, kp_ref, vc_ref, vp_ref, t0_ref, t1_ref, t2_ref,
                       o_ref, kk, vv, og, ls):
    s = pl.program_id(1)
    sb = SWA_SUPER
    kk[0:sb, :] = kp_ref[...]
    kk[sb:2 * sb, :] = kc_ref[...]
    vv[0:sb, :] = vp_ref[...]
    vv[sb:2 * sb, :] = vc_ref[...]
    col = lax.broadcasted_iota(I32, (SWA_BLK, 2 * SWA_BLK), 1)
    before_start = col < SWA_BLK

    for g, (_, d) in enumerate(DIL_GROUPS):
        q_ref = (q0_ref, q1_ref, q2_ref)[g]
        u_row = jnp.broadcast_to((t0_ref, t1_ref, t2_ref)[g][...], (SWA_BLK, SWA_TOEPLITZ_PERIOD))
        tab = pltpu.roll(u_row, 0, 1, stride=1, stride_axis=0)[:, :2 * SWA_BLK]
        n_mb = sb // (SWA_BLK * d)

        def block(idx, carry, q_ref=q_ref, tab=tab, d=d, n_mb=n_mb, g=g):
            r = idx // n_mb
            mb = idx % n_mb
            start = r + d * SWA_BLK * mb
            kstart = sb + start - d * SWA_BLK
            if d == 1:
                rows_q, rows_k = pl.ds(start, SWA_BLK), pl.ds(kstart, 2 * SWA_BLK)
            else:
                rows_q, rows_k = pl.ds(start, SWA_BLK, stride=d), pl.ds(kstart, 2 * SWA_BLK, stride=d)
            q = q_ref[rows_q, :].astype(BF16)
            k = kk[rows_k, :].astype(BF16)
            v = vv[rows_k, :].astype(BF16)
            lg = _dot_nt(q, k) * (SWA_SCALE * LOG2_E) + tab
            no_prev = jnp.logical_and(s == 0, mb == 0)
            lg = jnp.where(jnp.logical_and(no_prev, before_start), NEG_BIG, lg)
            m = jnp.max(lg, axis=-1, keepdims=True)
            p = jnp.exp2(lg - m)
            l = jnp.sum(p, axis=-1, keepdims=True)
            og[g, rows_q, :] = _dot(p.astype(BF16), v) / l
            ls[g, rows_q, :] = jnp.broadcast_to(m + jnp.log2(l), (SWA_BLK, HEAD_DIM))
            return carry

        lax.fori_loop(0, sb // SWA_BLK, block, 0, unroll=8)

    l0, l1, l2 = ls[0], ls[1], ls[2]
    mx = jnp.maximum(jnp.maximum(l0, l1), l2)
    e0, e1, e2 = jnp.exp2(l0 - mx), jnp.exp2(l1 - mx), jnp.exp2(l2 - mx)
    o_ref[...] = (e0 * og[0] + e1 * og[1] + e2 * og[2]) / (e0 + e1 + e2)


def _swa_prompt(proj, tables, n_p):
    sb = SWA_SUPER
    assert n_p % sb == 0
    kcol, vcol = SWA_NQ // HEAD_DIM, (SWA_NQ + SWA_NKV) // HEAD_DIM
    blk = lambda f: pl.BlockSpec((sb, HEAD_DIM), f)
    tspec = lambda g: pl.BlockSpec((None, 1, SWA_TOEPLITZ_PERIOD), lambda h, s: (g * SWA_HEADS + h, 0, 0))
    prev = lambda s: jnp.maximum(s - 1, 0)
    return pl.pallas_call(
        _swa_prompt_kernel,
        grid=(SWA_HEADS, n_p // sb),
        in_specs=[blk(lambda h, s: (s, h)), blk(lambda h, s: (s, SWA_HEADS + h)), blk(lambda h, s: (s, 2 * SWA_HEADS + h)),
                  blk(lambda h, s: (s, kcol + h)), blk(lambda h, s: (prev(s), kcol + h)),
                  blk(lambda h, s: (s, vcol + h)), blk(lambda h, s: (prev(s), vcol + h)),
                  tspec(0), tspec(1), tspec(2)],
        out_specs=blk(lambda h, s: (s, h)),
        out_shape=jax.ShapeDtypeStruct((n_p, SWA_NKV), F32),
        scratch_shapes=[pltpu.VMEM((2 * sb, HEAD_DIM), F32), pltpu.VMEM((2 * sb, HEAD_DIM), F32),
                        pltpu.VMEM((N_DIL, sb, HEAD_DIM), F32), pltpu.VMEM((N_DIL, sb, HEAD_DIM), F32)],
        compiler_params=_params("arbitrary", "arbitrary"),
        name="swa_prompt",
    )(proj, proj, proj, proj, proj, proj, proj, tables, tables, tables)


SWA_MB = 32
SWA_RES = 16
SWA_KEYS_PER_PASS = 16
LOG2_E = math.log2(math.e)


def _sample_bias_tables(biases, l_cache, t_s):
    b_near, b_mid, b_wide = biases
    n_groups = l_cache // SWA_RES
    neg = jnp.full((SWA_HEADS, 1), NEG_BIG, F32)

    def pick(b, j):
        j = np.asarray(j)
        ok = (j >= 0) & (j <= SWA_BLK)
        vals = jnp.concatenate([b, neg], axis=1)[:, np.where(ok, j, SWA_BLK + 1)]
        return jnp.moveaxis(vals, 0, -1)

    t_wide = pick(b_wide, n_groups - np.arange(n_groups))
    variant = np.arange(5)[:, None]
    t_mid = pick(b_mid, (SWA_BLK - 3 + variant) - 4 * np.arange(SWA_MB)[None, :])
    t_near = pick(b_near, SWA_BLK + 7 - np.arange(SWA_BLK + 7))
    d_new = np.arange(t_s)[:, None] - np.arange(t_s)[None, :]
    t_new = jnp.stack([pick(b, np.where((d_new >= 0) & (d_new % d == 0), d_new // d, -1))
                       for b, (_, d) in zip(biases, DIL_GROUPS)], axis=0)
    rep = lambda t: jnp.broadcast_to((t * LOG2_E)[..., None], t.shape + (HEAD_DIM,))
    return rep(t_wide), rep(t_mid), rep(t_near), rep(t_new)


def _swa_sample_kernel(q_ref, kn_ref, vn_ref, ka_ref, va_ref, kl_ref, vl_ref, kb_ref, vb_ref,
                       tw_ref, tm_ref, tnr_ref, tnew_ref, o_ref, m_s, l_s, acc, *, t_s, n_groups):
    c = pl.program_id(1)
    n_steps = pl.num_programs(1)
    mb = SWA_MB

    @pl.when(c == 0)
    def _():
        m_s[...] = jnp.full(m_s.shape, NEG_BIG, F32)
        l_s[...] = jnp.zeros(l_s.shape, F32)
        acc[...] = jnp.zeros(acc.shape, F32)

    def absorb(slot, q, kt, vt, bias):
        m_run, l_run, a_run = m_s[slot], l_s[slot], acc[slot]
        for lo in range(0, kt.shape[0], SWA_KEYS_PER_PASS):
            hi = min(lo + SWA_KEYS_PER_PASS, kt.shape[0])
            s = jnp.sum(q[None] * kt[lo:hi], axis=-1, keepdims=True) + bias[lo:hi]
            m_new = jnp.maximum(m_run, jnp.max(s, axis=0))
            alpha = jnp.exp2(m_run - m_new)
            p = jnp.exp2(s - m_new[None])
            l_run = alpha * l_run + jnp.sum(p, axis=0)
            a_run = alpha * a_run + jnp.sum(p * vt[lo:hi], axis=0)
            m_run = m_new
        m_s[slot], l_s[slot], acc[slot] = m_run, l_run, a_run

    def query(i, g):
        return q_ref[i, g] * (SWA_SCALE * LOG2_E)

    def wide(i, carry):
        absorb(2 * t_s + i, query(i, 2), ka_ref[:, i], va_ref[:, i], tw_ref[pl.ds(c * mb, mb)])
        return carry

    lax.fori_loop(0, t_s, wide, 0, unroll=True)

    @pl.when(c == 0)
    def _():
        def wide_last(i, carry):
            absorb(2 * t_s + i, query(i, 2), kl_ref[:, i], vl_ref[:, i], tw_ref[n_groups - mb:n_groups])
            return carry

        lax.fori_loop(0, t_s, wide_last, 0, unroll=True)

        def fresh(i, carry):
            for g in range(N_DIL):
                absorb(g * t_s + i, query(i, g), kn_ref[...], vn_ref[...], tnew_ref[g, i])
            return carry

        lax.fori_loop(0, t_s, fresh, 0, unroll=2)

    @pl.when(c == 1)
    def _():
        def mid(i, carry):
            rho, hi = i % 4, i // 4
            kts, vts, bs = [], [], []
            for rr in range(4):
                k_src, v_src = (kl_ref, vl_ref) if rr < 2 else (kb_ref, vb_ref)
                r_idx = rho + 4 * (rr % 2)
                kts.append(k_src[:, r_idx])
                vts.append(v_src[:, r_idx])
                bs.append(tm_ref[hi - rr + 3])
            absorb(t_s + i, query(i, 1), jnp.concatenate(kts, axis=0), jnp.concatenate(vts, axis=0),
                   jnp.concatenate(bs, axis=0))
            return carry

        lax.fori_loop(0, t_s, mid, 0, unroll=2)

    @pl.when(c == n_steps - 1)
    def _():
        near_groups = SWA_BLK // SWA_RES
        n_near = near_groups * (SWA_RES // 2)

        def near(i, carry):
            kts, vts, bs = [], [], []
            for k_src, v_src, off in ((kl_ref, vl_ref, 7), (kb_ref, vb_ref, 7 + SWA_RES // 2)):
                kts.append(k_src[mb - near_groups:mb].reshape(n_near, SWA_HEADS, HEAD_DIM))
                vts.append(v_src[mb - near_groups:mb].reshape(n_near, SWA_HEADS, HEAD_DIM))
                bs += [tnr_ref[pl.ds(SWA_RES * grp + off - i, SWA_RES // 2)] for grp in range(near_groups)]
            absorb(i, query(i, 0), jnp.concatenate(kts, axis=0), jnp.concatenate(vts, axis=0),
                   jnp.concatenate(bs, axis=0))
            return carry

        lax.fori_loop(0, t_s, near, 0, unroll=2)

        for i in range(t_s):
            lse = [m_s[g * t_s + i] + jnp.log2(l_s[g * t_s + i]) for g in range(N_DIL)]
            mx = jnp.maximum(jnp.maximum(lse[0], lse[1]), lse[2])
            e = [jnp.exp2(x - mx) for x in lse]
            num = sum(e[g] * (acc[g * t_s + i] / l_s[g * t_s + i]) for g in range(N_DIL))
            o_ref[i] = num / (e[0] + e[1] + e[2])


def _swa_sample(q_new, k_new, v_new, cache_k, cache_v, tables, n_b, t_s):
    l_cache = cache_k.shape[1]
    n_groups = l_cache // SWA_RES
    n_steps = n_groups // SWA_MB - 1
    assert t_s == SWA_RES // 2 and l_cache % (SWA_RES * SWA_MB) == 0 and l_cache >= SWA_MAX_WINDOW and n_steps >= 3
    ck = cache_k.reshape(n_b, n_groups, SWA_RES, SWA_HEADS, HEAD_DIM)
    cv = cache_v.reshape(n_b, n_groups, SWA_RES, SWA_HEADS, HEAD_DIM)
    half = (SWA_MB, SWA_RES // 2, SWA_HEADS, HEAD_DIM)
    a_spec = pl.BlockSpec((None,) + half, lambda b, c: (b, c, 0, 0, 0))
    last_spec = lambda r: pl.BlockSpec((None,) + half, lambda b, c: (b, n_steps, r, 0, 0))
    new_spec = pl.BlockSpec((None, t_s, SWA_HEADS, HEAD_DIM), lambda b, c: (b, 0, 0, 0))
    const = lambda t: pl.BlockSpec(t.shape, lambda b, c: (0,) * t.ndim)
    slots = N_DIL * t_s
    state = pltpu.VMEM((slots, SWA_HEADS, HEAD_DIM), F32)
    return pl.pallas_call(
        functools.partial(_swa_sample_kernel, t_s=t_s, n_groups=n_groups),
        grid=(n_b, n_steps),
        in_specs=[pl.BlockSpec((None, t_s, N_DIL, SWA_HEADS, HEAD_DIM), lambda b, c: (b, 0, 0, 0, 0)),
                  new_spec, new_spec, a_spec, a_spec, last_spec(0), last_spec(0), last_spec(1), last_spec(1)]
                 + [const(t) for t in tables],
        out_specs=new_spec,
        out_shape=jax.ShapeDtypeStruct((n_b, t_s, SWA_HEADS, HEAD_DIM), F32),
        scratch_shapes=[state, state, state],
        compiler_params=_params("arbitrary", "arbitrary"),
        name="swa_sample",
    )(q_new, k_new, v_new, ck, cv, ck, cv, ck, cv, *tables)


def _swa_mixer(x, n_p, n_b, t_s, cache_k, cache_v, w_in, w_out, rel_bias):
    tm = _pick_tile(math.gcd(n_p, n_b * t_s), PROJ_TM)
    proj = _matmul(x, w_in, SWA_NQ + 2 * SWA_NKV, tm, 1024)
    biases = _group_biases(rel_bias)
    o_p = _swa_prompt(proj, _prompt_bias_tables(biases), n_p)
    k_all = proj[:, SWA_NQ:SWA_NQ + SWA_NKV]
    v_all = proj[:, SWA_NQ + SWA_NKV:]
    per_head = (n_b, t_s, SWA_HEADS, HEAD_DIM)
    o_s = _swa_sample(proj[n_p:, :SWA_NQ].reshape(n_b, t_s, N_DIL, SWA_HEADS, HEAD_DIM),
                      k_all[n_p:].reshape(per_head), v_all[n_p:].reshape(per_head), cache_k, cache_v,
                      _sample_bias_tables(biases, cache_k.shape[1], t_s), n_b, t_s)
    y = _matmul((o_p, o_s.reshape(n_b * t_s, SWA_NKV)), w_out, D_MODEL, tm, 512)
    return y, k_all, v_all


def _pad_rows(w, rows):
    return jnp.zeros((rows, w.shape[1]), w.dtype).at[:w.shape[0]].set(w)


def _pad_cols(w, cols):
    return jnp.zeros((w.shape[0], cols), w.dtype).at[:, :w.shape[1]].set(w)


def _gla_mixer(x0, n_p, n_b, t_s, state0, w_in, w_gate, b_gate, norm_w, w_out):
    tm = _pick_tile(math.gcd(n_p, n_b * t_s), PROJ_TM)
    proj = _matmul(x0, w_in, GLA_MAIN, tm, 512)
    glow = _matmul(x0, _pad_cols(w_in[:, GLA_MAIN:], LANES), LANES, tm, LANES)
    wg_pad = _pad_rows(w_gate, LANES)
    bg = b_gate.reshape(1, GLA_DK)
    nw = norm_w.reshape(1, GLA_DV_HEAD)
    zero_state = jnp.zeros((1, GLA_HEADS, GLA_DK_HEAD, GLA_DV_HEAD), F32)
    o_p, st_p = _gla(proj, glow, wg_pad, bg, nw, zero_state, n_seq=1, seq_len=n_p, row_start=0,
                     rows_per_step=512, chunk=GLA_CHUNK, mxu_intra=True)
    o_s, st_s = _gla(proj, glow, wg_pad, bg, nw, state0, n_seq=n_b, seq_len=t_s, row_start=n_p,
                     rows_per_step=t_s, chunk=math.gcd(t_s, GLA_CHUNK), mxu_intra=False)
    return _matmul((o_p, o_s), w_out, D_MODEL, tm, 512), st_p, st_s


def kernel(x_prompt, x_sample, state_gla, cache_swa_k, cache_swa_v, gla_w_in, gla_w_gate, gla_b_gate, gla_norm_w,
           gla_w_out, swa_w_in, swa_w_out, rel_bias, w_router, b_router, moe_w_gate, moe_w_up, moe_w_down,
           ln1_g, ln1_b, ln2_g, ln2_b):
    n_p = x_prompt.shape[0] * x_prompt.shape[1]
    n_b, t_s = x_sample.shape[0], x_sample.shape[1]
    n_s = n_b * t_s
    assert x_prompt.shape[0] == 1, "one prompt sequence"
    x = (x_prompt.reshape(n_p, D_MODEL), x_sample.reshape(n_s, D_MODEL))
    w_router_t = w_router.T
    b_router_col = b_router.reshape(N_EXPERTS, 1)
    tm_ln = _pick_tile(math.gcd(n_p, n_s), LN_TM)

    gla_p, gla_s, kp, vp, ksm, vsm = [], [], [], [], [], []
    for i in range(DEPTH):
        j = i // 2
        if i % 2 == 0:
            y, st_p, st_s = _gla_mixer(x, n_p, n_b, t_s, state_gla[j], gla_w_in[j], gla_w_gate[j], gla_b_gate[j],
                                       gla_norm_w[j], gla_w_out[j])
            gla_p.append(st_p)
            gla_s.append(st_s)
        else:
            y, k_all, v_all = _swa_mixer(x, n_p, n_b, t_s, cache_swa_k[j], cache_swa_v[j], swa_w_in[j], swa_w_out[j],
                                         rel_bias)
            keep = min(SWA_MAX_WINDOW, n_p)
            kp.append(k_all[n_p - keep:n_p].reshape(1, keep, SWA_HEADS, HEAD_DIM))
            vp.append(v_all[n_p - keep:n_p].reshape(1, keep, SWA_HEADS, HEAD_DIM))
            ksm.append(k_all[n_p:].reshape(n_b, t_s, SWA_HEADS, HEAD_DIM))
            vsm.append(v_all[n_p:].reshape(n_b, t_s, SWA_HEADS, HEAD_DIM))
        x1, idx, gate, counts = _ln_route(x, y, ln1_g[i:i + 1], ln1_b[i:i + 1], w_router_t, b_router_col, tm_ln)
        last = i == DEPTH - 1
        x = _moe_ln(x1, idx, gate, counts, moe_w_gate, moe_w_up, moe_w_down, i, ln2_g[i:i + 1], ln2_b[i:i + 1],
                    n_split=n_p if last else None)
    y_p, y_s = x
    return (y_p.reshape(x_prompt.shape), y_s.reshape(x_sample.shape), jnp.stack(gla_p), jnp.stack(gla_s),
            jnp.stack(kp), jnp.stack(vp), jnp.stack(ksm), jnp.stack(vsm))
```

```python
import functools
import math

import numpy as np
import jax
import jax.numpy as jnp
from jax import lax
from jax.experimental import pallas as pl
from jax.experimental.pallas import tpu as pltpu

F32 = jnp.float32
BF16 = jnp.bfloat16
I32 = jnp.int32

D_MODEL = 2048
DEPTH = 2
GLA_HEADS = 4
GLA_DK_HEAD = 256
GLA_DV_HEAD = 512
GLA_DK = GLA_HEADS * GLA_DK_HEAD
GLA_DV = GLA_HEADS * GLA_DV_HEAD
GLA_GATE_RANK = 16
GLA_GATE_TAU = 16.0
GLA_CHUNK = 64
GLA_SUB = 16
GLA_SEQS_PER_STEP = 8
GLA_MAIN = 2 * GLA_DK + 2 * GLA_DV
RMS_EPS = 1e-6
HEAD_DIM = 128
SWA_HEADS = 16
DIL_GROUPS = ((128, 1), (512, 4), (2048, 16))
N_DIL = len(DIL_GROUPS)
SWA_NQ = N_DIL * SWA_HEADS * HEAD_DIM
SWA_NKV = SWA_HEADS * HEAD_DIM
SWA_MAX_WINDOW = 2048
NUM_BUCKETS = 32
MAX_DISTANCE = 2048
N_EXPERTS = 16
N_EXPERT_GROUPS = 4
EXPERTS_PER_GROUP = 4
D_EXPERT = 1024
DEEPNORM_ALPHA = (2 * DEPTH) ** 0.25
LN_EPS = 1e-5

LANES = 128
SUBLANES = 8
VMEM_LIMIT_BYTES = 56 * 1024 * 1024
NEG_BIG = -1e30


PROJ_TM = 1024
LN_TM = 256


def _pick_tile(n, preferred):
    t = preferred
    while t > SUBLANES and n % t:
        t //= 2
    assert n % t == 0, (n, t)
    return t


def _params(*sem):
    return pltpu.CompilerParams(dimension_semantics=sem, vmem_limit_bytes=VMEM_LIMIT_BYTES)


def _dot(a, b):
    return jnp.dot(a, b, preferred_element_type=F32)


def _dot_nt(a, b):
    return lax.dot_general(a, b, (((1,), (1,)), ((), ())), preferred_element_type=F32)


def _dot_tn(a, b):
    return lax.dot_general(a, b, (((0,), (0,)), ((), ())), preferred_element_type=F32)


def _split3(a):
    hi = a.astype(BF16)
    r1 = a - hi.astype(F32)
    mid = r1.astype(BF16)
    lo = (r1 - mid.astype(F32)).astype(BF16)
    return hi, mid, lo


def _row_segments(x):
    segs = tuple(x) if isinstance(x, (tuple, list)) else (x,)
    assert len(segs) in (1, 2)
    return segs


def _segment_specs(segs, tm):
    width = segs[0].shape[1]
    assert all(s.shape[0] % tm == 0 and s.shape[1] == width for s in segs)
    first = segs[0].shape[0] // tm
    if len(segs) == 1:
        return [pl.BlockSpec((tm, width), lambda i, *_: (i, 0))], first
    return [pl.BlockSpec((tm, width), lambda i, *_: (jnp.minimum(i, first - 1), 0)),
            pl.BlockSpec((tm, width), lambda i, *_: (jnp.maximum(i - first, 0), 0))], first


def _segment_tile(x_refs, first_tiles):
    if len(x_refs) == 1:
        return x_refs[0][...]
    return jnp.where(pl.program_id(0) < first_tiles, x_refs[0][...], x_refs[1][...])


def _mm_kernel(*refs, first_tiles):
    *x_refs, w_ref, o_ref, xb_ref = refs

    @pl.when(pl.program_id(1) == 0)
    def _():
        xb_ref[...] = _segment_tile(x_refs, first_tiles).astype(BF16)

    o_ref[...] = _dot(xb_ref[...], w_ref[...].astype(BF16)).astype(o_ref.dtype)


def _matmul(x, w, n_out, tm, tn):
    segs = _row_segments(x)
    m = sum(s.shape[0] for s in segs)
    k = segs[0].shape[1]
    assert n_out % tn == 0 and w.shape[0] == k
    x_specs, first = _segment_specs(segs, tm)
    return pl.pallas_call(
        functools.partial(_mm_kernel, first_tiles=first),
        grid=(m // tm, n_out // tn),
        in_specs=x_specs + [pl.BlockSpec((k, tn), lambda i, j: (0, j))],
        out_specs=pl.BlockSpec((tm, tn), lambda i, j: (i, j)),
        out_shape=jax.ShapeDtypeStruct((m, n_out), F32),
        scratch_shapes=[pltpu.VMEM((tm, k), BF16)],
        compiler_params=_params("arbitrary", "arbitrary"),
        name="dense_proj",
    )(*segs, w)


def _gla_kernel(q_ref, k_ref, v_ref, r_ref, gl_ref, wg_ref, bg_ref, nw_ref, s0_ref,
                o_ref, sf_ref, state, bsc, *, chunk, n_chunks, seqs, mxu_intra):
    c_sz = chunk
    r_sz = chunk * n_chunks
    t = pl.program_id(2)

    @pl.when(t == 0)
    def _():
        state[...] = s0_ref[...]

    rowi = lax.broadcasted_iota(I32, (c_sz, GLA_DK_HEAD), 0)
    tri = (lax.broadcasted_iota(I32, (c_sz, c_sz), 0) >= lax.broadcasted_iota(I32, (c_sz, c_sz), 1)).astype(BF16)
    ones = jnp.ones((c_sz, LANES), BF16)
    wg =wg_ref[...].astype(BF16)

    def one_chunk(sb, c):
        rows = pl.ds(pl.multiple_of(sb * r_sz + c * c_sz, c_sz), c_sz)
        q = q_ref[rows, :] * (GLA_DK_HEAD ** -0.5)
        k = k_ref[rows, :]
        v = v_ref[rows, :]
        z = _dot(gl_ref[rows, :].astype(BF16), wg) + bg_ref[...]
        la = (jnp.minimum(z, 0.0) - jnp.log(1.0 + jnp.exp(-jnp.abs(z)))) * (1.0 / GLA_GATE_TAU)
        l_hi, l_mid, l_lo = _split3(la)
        b = _dot(tri, l_hi) + (_dot(tri, l_mid) + _dot(tri, l_lo))
        dcol = _dot_tn(l_hi, ones) + (_dot_tn(l_mid, ones) + _dot_tn(l_lo, ones))
        b_last = b[c_sz - 1:c_sz, :]

        s_old = state[sb]
        o = _dot((q * jnp.exp(b)).astype(BF16), s_old.astype(BF16))

        if mxu_intra:
            n_sub = c_sz // GLA_SUB
            parts = [jnp.zeros((GLA_SUB, c_sz), F32)]
            colc = lax.broadcasted_iota(I32, (GLA_SUB, c_sz), 1)
            for blk in range(1, n_sub):
                lo = blk * GLA_SUB
                ref_b = b[lo - 1:lo, :]
                q_rel = q[lo:lo + GLA_SUB, :] * jnp.exp(b[lo:lo + GLA_SUB, :] - ref_b)
                k_rel = k * jnp.exp(jnp.minimum(ref_b - b, 0.0))
                s_blk = _dot_nt(q_rel.astype(BF16), k_rel.astype(BF16))
                parts.append(jnp.where(colc < lo, s_blk, 0.0))
            att = jnp.concatenate(parts, axis=0)

            b3 = b.reshape(n_sub, GLA_SUB, GLA_DK_HEAD)
            q3 = q.reshape(n_sub, GLA_SUB, GLA_DK_HEAD)
            k3 = k.reshape(n_sub, GLA_SUB, GLA_DK_HEAD)
            row3 = lax.broadcasted_iota(I32, b3.shape, 1)
            blk_lane0 = (lax.broadcasted_iota(I32, (c_sz, c_sz), 0) // GLA_SUB) * GLA_SUB
            lane_c = lax.broadcasted_iota(I32, (c_sz, c_sz), 1)
            for jj in range(GLA_SUB):
                e = jnp.exp(jnp.where(row3 >= jj, b3 - b3[:, jj:jj + 1, :], -jnp.inf))
                col = jnp.sum(q3 * e * k3[:, jj:jj + 1, :], axis=-1, keepdims=True).reshape(c_sz, 1)
                att = jnp.where(lane_c == blk_lane0 + jj, col, att)
            o = o + _dot(att.astype(BF16), v.astype(BF16))
        else:
            bsc[sb] = b
            qk_base = sb * r_sz + c * c_sz

            def jbody(j, acc):
                kj = k_ref[pl.ds(qk_base + j, 1), :]
                vj = v_ref[pl.ds(qk_base + j, 1), :]
                e = jnp.exp(jnp.where(rowi >= j, b - bsc[sb, pl.ds(j, 1), :], -jnp.inf))
                return acc + jnp.sum(q * e * kj, axis=-1, keepdims=True) * vj

            o = lax.fori_loop(0, c_sz, jbody, o, unroll=True)

        kt = (k * jnp.exp(b_last - b)).astype(BF16)
        decay = jnp.exp(dcol)
        ds = _dot_tn(kt, v.astype(BF16))
        state[sb] = s_old * jnp.tile(decay, (1, GLA_DV_HEAD // LANES)) + ds

        o = o * lax.rsqrt(jnp.mean(o * o, axis=-1, keepdims=True) + RMS_EPS) * nw_ref[...]
        r = r_ref[rows, :]
        o_ref[rows, :] = o * (r / (1.0 + jnp.exp(-r)))

    for sb in range(seqs):
        if n_chunks == 1:
            one_chunk(sb, 0)
        else:
            def body(c, carry, sb=sb):
                one_chunk(sb, c)
                return carry
            lax.fori_loop(0, n_chunks, body, 0, unroll=4)

    @pl.when(t == pl.num_programs(2) - 1)
    def _():
        sf_ref[...] = state[...]


def _gla(proj, glow, w_gate_pad, b_gate, norm_w, s0, *, n_seq, seq_len, row_start, rows_per_step, chunk,
         mxu_intra):
    r_sz = rows_per_step
    steps = seq_len // r_sz
    seqs = math.gcd(n_seq, GLA_SEQS_PER_STEP) if steps == 1 else 1
    blk_rows = seqs * r_sz
    assert seq_len % r_sz == 0 and r_sz % chunk == 0 and row_start % blk_rows == 0
    rb0 = row_start // blk_rows
    kq, kv = GLA_DK // GLA_DK_HEAD, GLA_DK // GLA_DV_HEAD

    def rowblk(b, t):
        return rb0 + b * steps + t

    st_spec = pl.BlockSpec((seqs, None, GLA_DK_HEAD, GLA_DV_HEAD), lambda b, h, t: (b, h, 0, 0))
    kern = functools.partial(_gla_kernel, chunk=chunk, n_chunks=r_sz // chunk, seqs=seqs, mxu_intra=mxu_intra)
    return pl.pallas_call(
        kern,
        grid=(n_seq // seqs, GLA_HEADS, steps),
        in_specs=[
            pl.BlockSpec((blk_rows, GLA_DK_HEAD), lambda b, h, t: (rowblk(b, t), h)),
            pl.BlockSpec((blk_rows, GLA_DK_HEAD), lambda b, h, t: (rowblk(b, t), kq + h)),
            pl.BlockSpec((blk_rows, GLA_DV_HEAD), lambda b, h, t: (rowblk(b, t), 2 * kv + h)),
            pl.BlockSpec((blk_rows, GLA_DV_HEAD), lambda b, h, t: (rowblk(b, t), 2 * kv + GLA_HEADS + h)),
            pl.BlockSpec((blk_rows, LANES), lambda b, h, t: (rowblk(b, t), 0)),
            pl.BlockSpec((LANES, GLA_DK_HEAD), lambda b, h, t: (0, h)),
            pl.BlockSpec((1, GLA_DK_HEAD), lambda b, h, t: (0, h)),
            pl.BlockSpec((1, GLA_DV_HEAD), lambda b, h, t: (0, 0)),
            st_spec,
        ],
        out_specs=[
            pl.BlockSpec((blk_rows, GLA_DV_HEAD), lambda b, h, t: (b * steps + t, h)),
            st_spec,
        ],
        out_shape=[jax.ShapeDtypeStruct((n_seq * seq_len, GLA_DV), F32),
                   jax.ShapeDtypeStruct((n_seq, GLA_HEADS, GLA_DK_HEAD, GLA_DV_HEAD), F32)],
        scratch_shapes=[pltpu.VMEM((seqs, GLA_DK_HEAD, GLA_DV_HEAD), F32),
                        pltpu.VMEM((seqs, chunk, GLA_DK_HEAD), F32)],
        compiler_params=_params("arbitrary", "arbitrary", "arbitrary"),
        name="gla_chunked",
    )(proj, proj, proj, proj, glow, w_gate_pad, b_gate, norm_w, s0)


def _layer_norm(x, g, b):
    mu = jnp.mean(x, axis=-1, keepdims=True)
    xc = x - mu
    var = jnp.mean(xc * xc, axis=-1, keepdims=True)
    return xc * lax.rsqrt(var + LN_EPS) * g + b


def _route(x, wr_ref, br_ref, idx_ref, gate_ref, cnt_ref, seen):
    tm = x.shape[0]
    xh, xm, xl = _split3(x)
    wh, wm, wl = _split3(wr_ref[...])
    logits = _dot_nt(wh, xh) + ((_dot_nt(wm, xh) + _dot_nt(wh, xm))
                                + ((_dot_nt(wm, xm) + _dot_nt(wl, xh)) + _dot_nt(wh, xl)))
    scores = 1.0 / (1.0 + jnp.exp(-logits))
    lane = lax.broadcasted_iota(I32, (N_EXPERTS, tm), 0).astype(F32)
    sel = scores + br_ref[...]
    big = float(N_EXPERTS)
    best = None
    for g in range(N_EXPERT_GROUPS):
        in_g = (lane >= g * EXPERTS_PER_GROUP) & (lane < (g + 1) * EXPERTS_PER_GROUP)
        vg = jnp.where(in_g, sel, -jnp.inf)
        m1 = jnp.max(vg, axis=0, keepdims=True)
        i1 = jnp.min(jnp.where(vg == m1, lane, big), axis=0, keepdims=True)
        vg2 = jnp.where(lane == i1, -jnp.inf, vg)
        m2 = jnp.max(vg2, axis=0, keepdims=True)
        i2 = jnp.min(jnp.where(vg2 == m2, lane, big), axis=0, keepdims=True)
        gs = m1 + m2
        if best is None:
            best, b1, b2 = gs, i1, i2
        else:
            better = gs > best
            best = jnp.where(better, gs, best)
            b1 = jnp.where(better, i1, b1)
            b2 = jnp.where(better, i2, b2)
    w1 = jnp.sum(jnp.where(lane == b1, scores, 0.0), axis=0, keepdims=True)
    w2 = jnp.sum(jnp.where(lane == b2, scores, 0.0), axis=0, keepdims=True)
    den = w1 + w2
    picked = jnp.logical_or(lane == b1, lane == b2).astype(F32)
    earlier = (lax.broadcasted_iota(I32, (tm, tm), 0) < lax.broadcasted_iota(I32, (tm, tm), 1)).astype(BF16)
    before = _dot(picked.astype(BF16), earlier) + seen[...]
    r1 = jnp.sum(jnp.where(lane == b1, before, 0.0), axis=0, keepdims=True)
    r2 = jnp.sum(jnp.where(lane == b2, before, 0.0), axis=0, keepdims=True)
    seen[...] = seen[...] + jnp.sum(picked, axis=1, keepdims=True)
    cnt_ref[...] = jnp.broadcast_to(seen[...], cnt_ref.shape)
    row = lax.broadcasted_iota(I32, (SUBLANES, tm), 0)
    idx_ref[...] = jnp.where(row == 0, b1, jnp.where(row == 1, b2, jnp.where(row == 2, r1, jnp.where(
        row == 3, r2, 0.0)))).astype(I32)
    gate_ref[...] = jnp.where(row == 0, w1 / den, jnp.where(row == 1, w2 / den, 0.0))


def _ln_route_kernel(*refs, first_tiles):
    *x_refs, y_ref, g_ref, b_ref, wr_ref, br_ref, o_ref, idx_ref, gate_ref, cnt_ref, seen = refs

    @pl.when(pl.program_id(0) == 0)
    def _():
        seen[...] = jnp.zeros(seen.shape, F32)

    x = _segment_tile(x_refs, first_tiles)
    x1 = _layer_norm(DEEPNORM_ALPHA * x + y_ref[...], g_ref[...], b_ref[...])
    o_ref[...] = x1
    _route(x1, wr_ref, br_ref, idx_ref, gate_ref, cnt_ref, seen)


def _ln_route(x, y, g, b, w_router_t, b_router_col, tm):
    segs = _row_segments(x)
    n, d = y.shape
    row = lambda i: (i, 0)
    col = lambda i: (0, i)
    fixed = lambda i: (0, 0)
    x_specs, first = _segment_specs(segs, tm)
    return pl.pallas_call(
        functools.partial(_ln_route_kernel, first_tiles=first),
        grid=(n // tm,),
        in_specs=x_specs + [pl.BlockSpec((tm, d), row),
                  pl.BlockSpec((1, d), fixed), pl.BlockSpec((1, d), fixed),
                  pl.BlockSpec((N_EXPERTS, d), fixed), pl.BlockSpec((N_EXPERTS, 1), fixed)],
        out_specs=[pl.BlockSpec((tm, d), row), pl.BlockSpec((SUBLANES, tm), col), pl.BlockSpec((SUBLANES, tm), col),
                   pl.BlockSpec((N_EXPERTS, LANES), fixed)],
        out_shape=[jax.ShapeDtypeStruct((n, d), F32), jax.ShapeDtypeStruct((SUBLANES, n), I32),
                   jax.ShapeDtypeStruct((SUBLANES, n), F32), jax.ShapeDtypeStruct((N_EXPERTS, LANES), F32)],
        scratch_shapes=[pltpu.VMEM((N_EXPERTS, 1), F32)],
        compiler_params=_params("arbitrary"),
        name="deepnorm_ln_router",
    )(*segs, y, g, b, w_router_t, b_router_col)


def _gmm1_kernel(src_ref, te_ref, ch_ref, nu_ref, x_hbm, wg_ref, wu_ref, h_ref, xbuf, wgb, wub, sems, *, tm):
    i = pl.program_id(0)
    n_used = nu_ref[0]

    def row_copy(tile, r, slot):
        return pltpu.make_async_copy(x_hbm.at[pl.ds(src_ref[tile * tm + r], 1), :],
                                     xbuf.at[slot, pl.ds(r, 1), :], sems.at[slot])

    def fetch(tile, slot):
        def issue(r, c):
            row_copy(tile, r, slot).start()
            return c
        lax.fori_loop(0, tm, issue, 0, unroll=DMA_UNROLL)

    def drain(tile, slot):
        def wait_row(r, c):
            row_copy(tile, r, slot).wait()
            return c
        lax.fori_loop(0, tm, wait_row, 0, unroll=DMA_UNROLL)

    @pl.when(i == 0)
    def _():
        fetch(0, 0)

    @pl.when(i < n_used)
    def _():
        slot = i % 2
        drain(i, slot)

        @pl.when(ch_ref[i] == 1)
        def _():
            wgb[...] = wg_ref[...].astype(BF16)
            wub[...] = wu_ref[...].astype(BF16)

        x = xbuf[slot].astype(BF16)
        a = _dot(x, wgb[...])
        for r in range(tm):
            row_copy(i + 1, r, 1 - slot).start()
        u = _dot(x, wub[...])
        h_ref[...] = ((a / (1.0 + jnp.exp(-a))) * u).astype(BF16)

    @pl.when(i >= n_used)
    def _():
        @pl.when(i == n_used)
        def _():
            drain(i, i % 2)

        h_ref[...] = jnp.zeros(h_ref.shape, h_ref.dtype)


def _gmm1(x, src_row, w_gate, w_up, layer, meta, tm):
    te, _, ch, nu = meta
    n_tiles = src_row.shape[0] // tm
    d = x.shape[1]
    f = w_gate.shape[-1]
    wspec = pl.BlockSpec((None, None, d, f), lambda i, src, te, ch, nu: (layer, te[i], 0, 0))
    return pl.pallas_call(
        functools.partial(_gmm1_kernel, tm=tm),
        grid_spec=pltpu.PrefetchScalarGridSpec(
            num_scalar_prefetch=4,
            grid=(n_tiles,),
            in_specs=[pl.BlockSpec(memory_space=pl.ANY), wspec, wspec],
            out_specs=pl.BlockSpec((tm, f), lambda i, src, te, ch, nu: (i, 0)),
            scratch_shapes=[pltpu.VMEM((2, tm, d), F32), pltpu.VMEM((d, f), BF16), pltpu.VMEM((d, f), BF16),
                            pltpu.SemaphoreType.DMA((2,))],
        ),
        out_shape=jax.ShapeDtypeStruct((n_tiles * tm, f), BF16),
        compiler_params=_params("arbitrary"),
        name="moe_gather_gate_up",
    )(src_row, te, ch, nu, x, w_gate, w_up)


def _gmm2_kernel(te_ref, tb_ref, ch_ref, nu_ref, h_ref, wd_ref, y_ref, wdb):
    i = pl.program_id(1)

    @pl.when(i < nu_ref[0])
    def _():
        @pl.when(ch_ref[i] == 1)
        def _():
            wdb[...] = wd_ref[...].astype(BF16)

        y_ref[...] = _dot(h_ref[...], wdb[...])

    @pl.when(i >= nu_ref[0])
    def _():
        y_ref[...] = jnp.zeros(y_ref.shape, y_ref.dtype)


def _gmm2(h, w_down, layer, meta, tm, tn):
    te, tb, ch, nu = meta
    n_tiles = h.shape[0] // tm
    f = h.shape[1]
    d = w_down.shape[-1]
    return pl.pallas_call(
        _gmm2_kernel,
        grid_spec=pltpu.PrefetchScalarGridSpec(
            num_scalar_prefetch=4,
            grid=(d // tn, n_tiles),
            in_specs=[pl.BlockSpec((tm, f), lambda j, i, te, tb, ch, nu: (tb[i], 0)),
                      pl.BlockSpec((None, None, f, tn), lambda j, i, te, tb, ch, nu: (layer, te[i], 0, j))],
            out_specs=pl.BlockSpec((tm, tn), lambda j, i, te, tb, ch, nu: (i, j)),
            scratch_shapes=[pltpu.VMEM((f, tn), BF16)],
        ),
        out_shape=jax.ShapeDtypeStruct((n_tiles * tm, d), F32),
        compiler_params=_params("arbitrary", "arbitrary"),
        name="moe_down",
    )(te, tb, ch, nu, h, w_down)


def _combine_ln_kernel(pos_ref, y_hbm, x_ref, gate_ref, g_ref, b_ref, *rest, tm, n_split_tiles):
    if n_split_tiles is None:
        (o_ref, ybuf, sems) = rest
    else:
        (op_ref, os_ref, ybuf, sems) = rest
    i = pl.program_id(0)

    def row_copy(tile, r, k, slot):
        return pltpu.make_async_copy(y_hbm.at[pl.ds(pos_ref[2 * (tile * tm + r) + k], 1), :],
                                     ybuf.at[slot, k, pl.ds(r, 1), :], sems.at[slot])

    def fetch(tile, slot):
        def issue(r, c):
            row_copy(tile, r, 0, slot).start()
            row_copy(tile, r, 1, slot).start()
            return c
        lax.fori_loop(0, tm, issue, 0, unroll=DMA_UNROLL)

    @pl.when(i == 0)
    def _():
        fetch(0, 0)

    @pl.when(i + 1 < pl.num_programs(0))
    def _():
        fetch(i + 1, (i + 1) % 2)

    slot = i % 2

    def drain(r, c):
        row_copy(i, r, 0, slot).wait()
        row_copy(i, r, 1, slot).wait()
        return c

    lax.fori_loop(0, tm, drain, 0, unroll=DMA_UNROLL)
    gate = gate_ref[...]
    moe = gate[:, 0:1] * ybuf[slot, 0] + gate[:, 1:2] * ybuf[slot, 1]
    out = _layer_norm(DEEPNORM_ALPHA * x_ref[...] + moe, g_ref[...], b_ref[...])
    if n_split_tiles is None:
        o_ref[...] = out
    else:
        @pl.when(i < n_split_tiles)
        def _():
            op_ref[...] = out

        @pl.when(i >= n_split_tiles)
        def _():
            os_ref[...] = out


def _combine_ln(y_sorted, pos, x, gate, g, b, tm, n_split=None):
    n, d = x.shape
    row = lambda i, pos: (i, 0)
    fixed = lambda i, pos: (0, 0)
    if n_split is None:
        nst = None
        out_specs = pl.BlockSpec((tm, d), row)
        out_shape = jax.ShapeDtypeStruct((n, d), F32)
    else:
        nst = n_split // tm
        out_specs = [pl.BlockSpec((tm, d), lambda i, pos: (jnp.minimum(i, nst - 1), 0)),
                     pl.BlockSpec((tm, d), lambda i, pos: (jnp.maximum(i - nst, 0), 0))]
        out_shape = [jax.ShapeDtypeStruct((n_split, d), F32), jax.ShapeDtypeStruct((n - n_split, d), F32)]
    return pl.pallas_call(
        functools.partial(_combine_ln_kernel, tm=tm, n_split_tiles=nst),
        grid_spec=pltpu.PrefetchScalarGridSpec(
            num_scalar_prefetch=1,
            grid=(n // tm,),
            in_specs=[pl.BlockSpec(memory_space=pl.ANY), pl.BlockSpec((tm, d), row),
                      pl.BlockSpec((tm, 2), row), pl.BlockSpec((1, d), fixed), pl.BlockSpec((1, d), fixed)],
            out_specs=out_specs,
            scratch_shapes=[pltpu.VMEM((2, 2, tm, d), F32), pltpu.SemaphoreType.DMA((2,))],
        ),
        out_shape=out_shape,
        compiler_params=_params("arbitrary"),
        name="moe_combine_ln",
    )(pos, y_sorted, x, gate, g, b)


def _dispatch_meta(idx, counts, tm, n_tiles):
    n = idx.shape[1]
    e_flat = idx[0:2].T.reshape(-1)
    rank = idx[2:4].T.reshape(-1)
    counts = counts[:, 0].astype(I32)
    ptiles = (counts + tm - 1) // tm
    tile_end = jnp.cumsum(ptiles)
    n_used = tile_end[-1]
    row_off = (tile_end - ptiles) * tm
    pos = row_off[e_flat] + rank
    tok = jnp.arange(2 * n, dtype=I32) // 2
    src_row = jnp.zeros((n_tiles * tm,), I32).at[pos].set(tok)
    tile_id = jnp.minimum(jnp.arange(n_tiles, dtype=I32), n_used - 1)
    tile_expert = jnp.sum((tile_id[:, None] >= tile_end[None, :]).astype(I32), axis=1)
    changed = jnp.concatenate([jnp.ones((1,), I32), (tile_expert[1:] != tile_expert[:-1]).astype(I32)])
    nu = n_used.reshape(1).astype(I32)
    return src_row, pos.astype(I32), (tile_expert, tile_id.astype(I32), changed, nu)


DMA_UNROLL = 8
MOE_TM = 256
MOE_TN = 2048


def _moe_ln(x1, idx, gate, counts, w_gate, w_up, w_down, layer, g, b, n_split=None):
    n = x1.shape[0]
    n_tiles = -(-2 * n // MOE_TM) + N_EXPERTS + 1
    src_row, pos, meta = _dispatch_meta(idx, counts, MOE_TM, n_tiles)
    h = _gmm1(x1, src_row, w_gate, w_up, layer, meta, MOE_TM)
    y = _gmm2(h, w_down, layer, meta, MOE_TM, MOE_TN)
    tm = _pick_tile(n if n_split is None else math.gcd(n, n_split), LN_TM)
    return _combine_ln(y, pos, x1, gate[0:2].T, g, b, tm, n_split)


SWA_BLK = 128
SWA_SUPER = SWA_BLK * 16
SWA_SCALE = HEAD_DIM ** -0.5
SWA_TOEPLITZ_PERIOD = 4 * SWA_BLK


def _t5_bucket(dist):
    max_exact = NUM_BUCKETS // 2
    d = jnp.maximum(dist.astype(F32), 1.0)
    large = max_exact + (jnp.log(d / max_exact) / math.log(MAX_DISTANCE / max_exact)
                         * (NUM_BUCKETS - max_exact)).astype(I32)
    large = jnp.minimum(large, NUM_BUCKETS - 1)
    return jnp.where(dist < max_exact, dist, large)


def _group_biases(rel_bias):
    out = []
    for g, (w, d) in enumerate(DIL_GROUPS):
        dist = d * jnp.arange(w // d + 1, dtype=I32)
        b = rel_bias[_t5_bucket(dist)]
        out.append(b[:, g * SWA_HEADS:(g + 1) * SWA_HEADS].T.astype(F32))
    return out


def _prompt_bias_tables(biases):
    b_all = jnp.concatenate(biases, axis=0)
    mask = jnp.full((b_all.shape[0], SWA_TOEPLITZ_PERIOD - SWA_BLK - 1), NEG_BIG, F32)
    return jnp.concatenate([b_all[:, ::-1] * LOG2_E, mask], axis=1)[:, None, :]


def _swa_prompt_kernel(q0_ref, q1_ref, q2_ref, kc_ref, kp_ref, vc_ref, vp_ref, t0_ref, t1_ref, t2_ref,
                       o_ref, kk, vv, og, ls):
    s = pl.program_id(1)
    sb = SWA_SUPER
    kk[0:sb, :] = kp_ref[...]
    kk[sb:2 * sb, :] = kc_ref[...]
    vv[0:sb, :] = vp_ref[...]
    vv[sb:2 * sb, :] = vc_ref[...]
    col = lax.broadcasted_iota(I32, (SWA_BLK, 2 * SWA_BLK), 1)
    before_start = col < SWA_BLK

    for g, (_, d) in enumerate(DIL_GROUPS):
        q_ref = (q0_ref, q1_ref, q2_ref)[g]
        u_row = jnp.broadcast_to((t0_ref, t1_ref, t2_ref)[g][...], (SWA_BLK, SWA_TOEPLITZ_PERIOD))
        tab = pltpu.roll(u_row, 0, 1, stride=1, stride_axis=0)[:, :2 * SWA_BLK]
        n_mb = sb // (SWA_BLK * d)

        def block(idx, carry, q_ref=q_ref, tab=tab, d=d, n_mb=n_mb, g=g):
            r = idx // n_mb
            mb = idx % n_mb
            start = r + d * SWA_BLK * mb
            kstart = sb + start - d * SWA_BLK
            if d == 1:
                rows_q, rows_k = pl.ds(start, SWA_BLK), pl.ds(kstart, 2 * SWA_BLK)
            else:
                rows_q, rows_k = pl.ds(start, SWA_BLK, stride=d), pl.ds(kstart, 2 * SWA_BLK, stride=d)
            q = q_ref[rows_q, :].astype(BF16)
            k = kk[rows_k, :].astype(BF16)
            v = vv[rows_k, :].astype(BF16)
            lg = _dot_nt(q, k) * (SWA_SCALE * LOG2_E) + tab
            no_prev = jnp.logical_and(s == 0, mb == 0)
            lg = jnp.where(jnp.logical_and(no_prev, before_start), NEG_BIG, lg)
            m = jnp.max(lg, axis=-1, keepdims=True)
            p = jnp.exp2(lg - m)
            l = jnp.sum(p, axis=-1, keepdims=True)
            og[g, rows_q, :] = _dot(p.astype(BF16), v) / l
            ls[g, rows_q, :] = jnp.broadcast_to(m + jnp.log2(l), (SWA_BLK, HEAD_DIM))
            return carry

        lax.fori_loop(0, sb // SWA_BLK, block, 0, unroll=True)

    l0, l1, l2 = ls[0], ls[1], ls[2]
    mx = jnp.maximum(jnp.maximum(l0, l1), l2)
    e0, e1, e2 = jnp.exp2(l0 - mx), jnp.exp2(l1 - mx), jnp.exp2(l2 - mx)
    o_ref[...] = (e0 * og[0] + e1 * og[1] + e2 * og[2]) / (e0 + e1 + e2)


def _swa_prompt(proj, tables, n_p):
    sb = SWA_SUPER
    assert n_p % sb == 0
    kcol, vcol = SWA_NQ // HEAD_DIM, (SWA_NQ + SWA_NKV) // HEAD_DIM
    blk = lambda f: pl.BlockSpec((sb, HEAD_DIM), f)
    tspec = lambda g: pl.BlockSpec((None, 1, SWA_TOEPLITZ_PERIOD), lambda h, s: (g * SWA_HEADS + h, 0, 0))
    prev = lambda s: jnp.maximum(s - 1, 0)
    return pl.pallas_call(
        _swa_prompt_kernel,
        grid=(SWA_HEADS, n_p // sb),
        in_specs=[blk(lambda h, s: (s, h)), blk(lambda h, s: (s, SWA_HEADS + h)), blk(lambda h, s: (s, 2 * SWA_HEADS + h)),
                  blk(lambda h, s: (s, kcol + h)), blk(lambda h, s: (prev(s), kcol + h)),
                  blk(lambda h, s: (s, vcol + h)), blk(lambda h, s: (prev(s), vcol + h)),
                  tspec(0), tspec(1), tspec(2)],
        out_specs=blk(lambda h, s: (s, h)),
        out_shape=jax.ShapeDtypeStruct((n_p, SWA_NKV), F32),
        scratch_shapes=[pltpu.VMEM((2 * sb, HEAD_DIM), F32), pltpu.VMEM((2 * sb, HEAD_DIM), F32),
                        pltpu.VMEM((N_DIL, sb, HEAD_DIM), F32), pltpu.VMEM((N_DIL, sb, HEAD_DIM), F32)],
        compiler_params=_params("arbitrary", "arbitrary"),
        name="swa_prompt",
    )(proj, proj, proj, proj, proj, proj, proj, tables, tables, tables)


SWA_MB = 32
SWA_RES = 16
SWA_KEYS_PER_PASS = 16
LOG2_E = math.log2(math.e)


def _sample_bias_tables(biases, l_cache, t_s):
    b_near, b_mid, b_wide = biases
    n_groups = l_cache // SWA_RES
    neg = jnp.full((SWA_HEADS, 1), NEG_BIG, F32)

    def pick(b, j):
        j = np.asarray(j)
        ok = (j >= 0) & (j <= SWA_BLK)
        vals = jnp.concatenate([b, neg], axis=1)[:, np.where(ok, j, SWA_BLK + 1)]
        return jnp.moveaxis(vals, 0, -1)

    t_wide = pick(b_wide, n_groups - np.arange(n_groups))
    variant = np.arange(5)[:, None]
    t_mid = pick(b_mid, (SWA_BLK - 3 + variant) - 4 * np.arange(SWA_MB)[None, :])
    t_near = pick(b_near, SWA_BLK + 7 - np.arange(SWA_BLK + 7))
    d_new = np.arange(t_s)[:, None] - np.arange(t_s)[None, :]
    t_new = jnp.stack([pick(b, np.where((d_new >= 0) & (d_new % d == 0), d_new // d, -1))
                       for b, (_, d) in zip(biases, DIL_GROUPS)], axis=0)
    rep = lambda t: jnp.broadcast_to((t * LOG2_E)[..., None], t.shape + (HEAD_DIM,))
    return rep(t_wide), rep(t_mid), rep(t_near), rep(t_new)


def _swa_sample_kernel(q_ref, kn_ref, vn_ref, ka_ref, va_ref, kl_ref, vl_ref, kb_ref, vb_ref,
                       tw_ref, tm_ref, tnr_ref, tnew_ref, o_ref, m_s, l_s, acc, *, t_s, n_groups):
    c = pl.program_id(1)
    n_steps = pl.num_programs(1)
    mb = SWA_MB

    @pl.when(c == 0)
    def _():
        m_s[...] = jnp.full(m_s.shape, NEG_BIG, F32)
        l_s[...] = jnp.zeros(l_s.shape, F32)
        acc[...] = jnp.zeros(acc.shape, F32)

    def absorb(slot, q, kt, vt, bias):
        m_run, l_run, a_run = m_s[slot], l_s[slot], acc[slot]
        for lo in range(0, kt.shape[0], SWA_KEYS_PER_PASS):
            hi = min(lo + SWA_KEYS_PER_PASS, kt.shape[0])
            s = jnp.sum(q[None] * kt[lo:hi], axis=-1, keepdims=True) + bias[lo:hi]
            m_new = jnp.maximum(m_run, jnp.max(s, axis=0))
            alpha = jnp.exp2(m_run - m_new)
            p = jnp.exp2(s - m_new[None])
            l_run = alpha * l_run + jnp.sum(p, axis=0)
            a_run = alpha * a_run + jnp.sum(p * vt[lo:hi], axis=0)
            m_run = m_new
        m_s[slot], l_s[slot], acc[slot] = m_run, l_run, a_run

    def query(i, g):
        return q_ref[i, g] * (SWA_SCALE * LOG2_E)

    def wide(i, carry):
        absorb(2 * t_s + i, query(i, 2), ka_ref[:, i], va_ref[:, i], tw_ref[pl.ds(c * mb, mb)])
        return carry

    lax.fori_loop(0, t_s, wide, 0, unroll=True)

    @pl.when(c == 0)
    def _():
        def wide_last(i, carry):
            absorb(2 * t_s + i, query(i, 2), kl_ref[:, i], vl_ref[:, i], tw_ref[n_groups - mb:n_groups])
            return carry

        lax.fori_loop(0, t_s, wide_last, 0, unroll=True)

        def fresh(i, carry):
            for g in range(N_DIL):
                absorb(g * t_s + i, query(i, g), kn_ref[...], vn_ref[...], tnew_ref[g, i])
            return carry

        lax.fori_loop(0, t_s, fresh, 0, unroll=2)

    @pl.when(c == 1)
    def _():
        def mid(i, carry):
            rho, hi = i % 4, i // 4
            kts, vts, bs = [], [], []
            for rr in range(4):
                k_src, v_src = (kl_ref, vl_ref) if rr < 2 else (kb_ref, vb_ref)
                r_idx = rho + 4 * (rr % 2)
                kts.append(k_src[:, r_idx])
                vts.append(v_src[:, r_idx])
                bs.append(tm_ref[hi - rr + 3])
            absorb(t_s + i, query(i, 1), jnp.concatenate(kts, axis=0), jnp.concatenate(vts, axis=0),
                   jnp.concatenate(bs, axis=0))
            return carry

        lax.fori_loop(0, t_s, mid, 0, unroll=2)

    @pl.when(c == n_steps - 1)
    def _():
        near_groups = SWA_BLK // SWA_RES
        n_near = near_groups * (SWA_RES // 2)

        def near(i, carry):
            kts, vts, bs = [], [], []
            for k_src, v_src, off in ((kl_ref, vl_ref, 7), (kb_ref, vb_ref, 7 + SWA_RES // 2)):
                kts.append(k_src[mb - near_groups:mb].reshape(n_near, SWA_HEADS, HEAD_DIM))
                vts.append(v_src[mb - near_groups:mb].reshape(n_near, SWA_HEADS, HEAD_DIM))
                bs += [tnr_ref[pl.ds(SWA_RES * grp + off - i, SWA_RES // 2)] for grp in range(near_groups)]
            absorb(i, query(i, 0), jnp.concatenate(kts, axis=0), jnp.concatenate(vts, axis=0),
                   jnp.concatenate(bs, axis=0))
            return carry

        lax.fori_loop(0, t_s, near, 0, unroll=2)

        for i in range(t_s):
            lse = [m_s[g * t_s + i] + jnp.log2(l_s[g * t_s + i]) for g in range(N_DIL)]
            mx = jnp.maximum(jnp.maximum(lse[0], lse[1]), lse[2])
            e = [jnp.exp2(x - mx) for x in lse]
            num = sum(e[g] * (acc[g * t_s + i] / l_s[g * t_s + i]) for g in range(N_DIL))
            o_ref[i] = num / (e[0] + e[1] + e[2])


def _swa_sample(q_new, k_new, v_new, cache_k, cache_v, tables, n_b, t_s):
    l_cache = cache_k.shape[1]
    n_groups = l_cache // SWA_RES
    n_steps = n_groups // SWA_MB - 1
    assert t_s == SWA_RES // 2 and l_cache % (SWA_RES * SWA_MB) == 0 and l_cache >= SWA_MAX_WINDOW and n_steps >= 3
    ck = cache_k.reshape(n_b, n_groups, SWA_RES, SWA_HEADS, HEAD_DIM)
    cv = cache_v.reshape(n_b, n_groups, SWA_RES, SWA_HEADS, HEAD_DIM)
    half = (SWA_MB, SWA_RES // 2, SWA_HEADS, HEAD_DIM)
    a_spec = pl.BlockSpec((None,) + half, lambda b, c: (b, c, 0, 0, 0))
    last_spec = lambda r: pl.BlockSpec((None,) + half, lambda b, c: (b, n_steps, r, 0, 0))
    new_spec = pl.BlockSpec((None, t_s, SWA_HEADS, HEAD_DIM), lambda b, c: (b, 0, 0, 0))
    const = lambda t: pl.BlockSpec(t.shape, lambda b, c: (0,) * t.ndim)
    slots = N_DIL * t_s
    state = pltpu.VMEM((slots, SWA_HEADS, HEAD_DIM), F32)
    return pl.pallas_call(
        functools.partial(_swa_sample_kernel, t_s=t_s, n_groups=n_groups),
        grid=(n_b, n_steps),
        in_specs=[pl.BlockSpec((None, t_s, N_DIL, SWA_HEADS, HEAD_DIM), lambda b, c: (b, 0, 0, 0, 0)),
                  new_spec, new_spec, a_spec, a_spec, last_spec(0), last_spec(0), last_spec(1), last_spec(1)]
                 + [const(t) for t in tables],
        out_specs=new_spec,
        out_shape=jax.ShapeDtypeStruct((n_b, t_s, SWA_HEADS, HEAD_DIM), F32),
        scratch_shapes=[state, state, state],
        compiler_params=_params("arbitrary", "arbitrary"),
        name="swa_sample",
    )(q_new, k_new, v_new, ck, cv, ck, cv, ck, cv, *tables)


def _swa_mixer(x, n_p, n_b, t_s, cache_k, cache_v, w_in, w_out, rel_bias):
    tm = _pick_tile(math.gcd(n_p, n_b * t_s), PROJ_TM)
    proj = _matmul(x, w_in, SWA_NQ + 2 * SWA_NKV, tm, 1024)
    biases = _group_biases(rel_bias)
    o_p = _swa_prompt(proj, _prompt_bias_tables(biases), n_p)
    k_all = proj[:, SWA_NQ:SWA_NQ + SWA_NKV]
    v_all = proj[:, SWA_NQ + SWA_NKV:]
    per_head = (n_b, t_s, SWA_HEADS, HEAD_DIM)
    o_s = _swa_sample(proj[n_p:, :SWA_NQ].reshape(n_b, t_s, N_DIL, SWA_HEADS, HEAD_DIM),
                      k_all[n_p:].reshape(per_head), v_all[n_p:].reshape(per_head), cache_k, cache_v,
                      _sample_bias_tables(biases, cache_k.shape[1], t_s), n_b, t_s)
    y = _matmul((o_p, o_s.reshape(n_b * t_s, SWA_NKV)), w_out, D_MODEL, tm, 512)
    return y, k_all, v_all


def _pad_rows(w, rows):
    return jnp.zeros((rows, w.shape[1]), w.dtype).at[:w.shape[0]].set(w)


def _pad_cols(w, cols):
    return jnp.zeros((w.shape[0], cols), w.dtype).at[:, :w.shape[1]].set(w)


def _gla_mixer(x0, n_p, n_b, t_s, state0, w_in, w_gate, b_gate, norm_w, w_out):
    tm = _pick_tile(math.gcd(n_p, n_b * t_s), PROJ_TM)
    proj = _matmul(x0, w_in, GLA_MAIN, tm, 512)
    glow = _matmul(x0, _pad_cols(w_in[:, GLA_MAIN:], LANES), LANES, tm, LANES)
    wg_pad = _pad_rows(w_gate, LANES)
    bg = b_gate.reshape(1, GLA_DK)
    nw = norm_w.reshape(1, GLA_DV_HEAD)
    zero_state = jnp.zeros((1, GLA_HEADS, GLA_DK_HEAD, GLA_DV_HEAD), F32)
    o_p, st_p = _gla(proj, glow, wg_pad, bg, nw, zero_state, n_seq=1, seq_len=n_p, row_start=0,
                     rows_per_step=512, chunk=GLA_CHUNK, mxu_intra=True)
    o_s, st_s = _gla(proj, glow, wg_pad, bg, nw, state0, n_seq=n_b, seq_len=t_s, row_start=n_p,
                     rows_per_step=t_s, chunk=math.gcd(t_s, GLA_CHUNK), mxu_intra=False)
    return _matmul((o_p, o_s), w_out, D_MODEL, tm, 512), st_p, st_s


def kernel(x_prompt, x_sample, state_gla, cache_swa_k, cache_swa_v, gla_w_in, gla_w_gate, gla_b_gate, gla_norm_w,
           gla_w_out, swa_w_in, swa_w_out, rel_bias, w_router, b_router, moe_w_gate, moe_w_up, moe_w_down,
           ln1_g, ln1_b, ln2_g, ln2_b):
    n_p = x_prompt.shape[0] * x_prompt.shape[1]
    n_b, t_s = x_sample.shape[0], x_sample.shape[1]
    n_s = n_b * t_s
    assert x_prompt.shape[0] == 1, "one prompt sequence"
    x = (x_prompt.reshape(n_p, D_MODEL), x_sample.reshape(n_s, D_MODEL))
    w_router_t = w_router.T
    b_router_col = b_router.reshape(N_EXPERTS, 1)
    tm_ln = _pick_tile(math.gcd(n_p, n_s), LN_TM)

    gla_p, gla_s, kp, vp, ksm, vsm = [], [], [], [], [], []
    for i in range(DEPTH):
        j = i // 2
        if i % 2 == 0:
            y, st_p, st_s = _gla_mixer(x, n_p, n_b, t_s, state_gla[j], gla_w_in[j], gla_w_gate[j], gla_b_gate[j],
                                       gla_norm_w[j], gla_w_out[j])
            gla_p.append(st_p)
            gla_s.append(st_s)
        else:
            y, k_all, v_all = _swa_mixer(x, n_p, n_b, t_s, cache_swa_k[j], cache_swa_v[j], swa_w_in[j], swa_w_out[j],
                                         rel_bias)
            keep = min(SWA_MAX_WINDOW, n_p)
            kp.append(k_all[n_p - keep:n_p].reshape(1, keep, SWA_HEADS, HEAD_DIM))
            vp.append(v_all[n_p - keep:n_p].reshape(1, keep, SWA_HEADS, HEAD_DIM))
            ksm.append(k_all[n_p:].reshape(n_b, t_s, SWA_HEADS, HEAD_DIM))
            vsm.append(v_all[n_p:].reshape(n_b, t_s, SWA_HEADS, HEAD_DIM))
        x1, idx, gate, counts = _ln_route(x, y, ln1_g[i:i + 1], ln1_b[i:i + 1], w_router_t, b_router_col, tm_ln)
        last = i == DEPTH - 1
        x = _moe_ln(x1, idx, gate, counts, moe_w_gate, moe_w_up, moe_w_down, i, ln2_g[i:i + 1], ln2_b[i:i + 1],
                    n_split=n_p if last else None)
    y_p, y_s = x
    return (y_p.reshape(x_prompt.shape), y_s.reshape(x_sample.shape), jnp.stack(gla_p), jnp.stack(gla_s),
            jnp.stack(kp), jnp.stack(vp), jnp.stack(ksm), jnp.stack(vsm))
```

```python
import functools
import math

import numpy as np
import jax
import jax.numpy as jnp
from jax import lax
from jax.experimental import pallas as pl
from jax.experimental.pallas import tpu as pltpu

F32 = jnp.float32
BF16 = jnp.bfloat16
I32 = jnp.int32

D_MODEL = 2048
DEPTH = 2
GLA_HEADS = 4
GLA_DK_HEAD = 256
GLA_DV_HEAD = 512
GLA_DK = GLA_HEADS * GLA_DK_HEAD
GLA_DV = GLA_HEADS * GLA_DV_HEAD
GLA_GATE_RANK = 16
GLA_GATE_TAU = 16.0
GLA_CHUNK = 64
GLA_SUB = 16
GLA_SEQS_PER_STEP = 8
GLA_MAIN = 2 * GLA_DK + 2 * GLA_DV
RMS_EPS = 1e-6
HEAD_DIM = 128
SWA_HEADS = 16
DIL_GROUPS = ((128, 1), (512, 4), (2048, 16))
N_DIL = len(DIL_GROUPS)
SWA_NQ = N_DIL * SWA_HEADS * HEAD_DIM
SWA_NKV = SWA_HEADS * HEAD_DIM
SWA_MAX_WINDOW = 2048
NUM_BUCKETS = 32
MAX_DISTANCE = 2048
N_EXPERTS = 16
N_EXPERT_GROUPS = 4
EXPERTS_PER_GROUP = 4
D_EXPERT = 1024
DEEPNORM_ALPHA = (2 * DEPTH) ** 0.25
LN_EPS = 1e-5

LANES = 128
SUBLANES = 8
VMEM_LIMIT_BYTES = 56 * 1024 * 1024
NEG_BIG = -1e30


PROJ_TM = 1024
LN_TM = 256


def _pick_tile(n, preferred):
    t = preferred
    while t > SUBLANES and n % t:
        t //= 2
    assert n % t == 0, (n, t)
    return t


def _params(*sem):
    return pltpu.CompilerParams(dimension_semantics=sem, vmem_limit_bytes=VMEM_LIMIT_BYTES)


def _dot(a, b):
    return jnp.dot(a, b, preferred_element_type=F32)


def _dot_nt(a, b):
    return lax.dot_general(a, b, (((1,), (1,)), ((), ())), preferred_element_type=F32)


def _dot_tn(a, b):
    return lax.dot_general(a, b, (((0,), (0,)), ((), ())), preferred_element_type=F32)


def _split3(a):
    hi = a.astype(BF16)
    r1 = a - hi.astype(F32)
    mid = r1.astype(BF16)
    lo = (r1 - mid.astype(F32)).astype(BF16)
    return hi, mid, lo


def _row_segments(x):
    segs = tuple(x) if isinstance(x, (tuple, list)) else (x,)
    assert len(segs) in (1, 2)
    return segs


def _segment_specs(segs, tm):
    width = segs[0].shape[1]
    assert all(s.shape[0] % tm == 0 and s.shape[1] == width for s in segs)
    first = segs[0].shape[0] // tm
    if len(segs) == 1:
        return [pl.BlockSpec((tm, width), lambda i, *_: (i, 0))], first
    return [pl.BlockSpec((tm, width), lambda i, *_: (jnp.minimum(i, first - 1), 0)),
            pl.BlockSpec((tm, width), lambda i, *_: (jnp.maximum(i - first, 0), 0))], first


def _segment_tile(x_refs, first_tiles):
    if len(x_refs) == 1:
        return x_refs[0][...]
    return jnp.where(pl.program_id(0) < first_tiles, x_refs[0][...], x_refs[1][...])


def _mm_kernel(*refs, first_tiles):
    *x_refs, w_ref, o_ref, xb_ref = refs

    @pl.when(pl.program_id(1) == 0)
    def _():
        xb_ref[...] = _segment_tile(x_refs, first_tiles).astype(BF16)

    o_ref[...] = _dot(xb_ref[...], w_ref[...].astype(BF16)).astype(o_ref.dtype)


def _matmul(x, w, n_out, tm, tn):
    segs = _row_segments(x)
    m = sum(s.shape[0] for s in segs)
    k = segs[0].shape[1]
    assert n_out % tn == 0 and w.shape[0] == k
    x_specs, first = _segment_specs(segs, tm)
    return pl.pallas_call(
        functools.partial(_mm_kernel, first_tiles=first),
        grid=(m // tm, n_out // tn),
        in_specs=x_specs + [pl.BlockSpec((k, tn), lambda i, j: (0, j))],
        out_specs=pl.BlockSpec((tm, tn), lambda i, j: (i, j)),
        out_shape=jax.ShapeDtypeStruct((m, n_out), F32),
        scratch_shapes=[pltpu.VMEM((tm, k), BF16)],
        compiler_params=_params("arbitrary", "arbitrary"),
        name="dense_proj",
    )(*segs, w)


def _gla_kernel(q_ref, k_ref, v_ref, r_ref, gl_ref, wg_ref, bg_ref, nw_ref, s0_ref,
                o_ref, sf_ref, state, bsc, *, chunk, n_chunks, seqs, mxu_intra):
    c_sz = chunk
    r_sz = chunk * n_chunks
    t = pl.program_id(2)

    @pl.when(t == 0)
    def _():
        state[...] = s0_ref[...]

    rowi = lax.broadcasted_iota(I32, (c_sz, GLA_DK_HEAD), 0)
    tri = (lax.broadcasted_iota(I32, (c_sz, c_sz), 0) >= lax.broadcasted_iota(I32, (c_sz, c_sz), 1)).astype(BF16)
    ones = jnp.ones((c_sz, LANES), BF16)
    wg =wg_ref[...].astype(BF16)

    def one_chunk(sb, c):
        rows = pl.ds(pl.multiple_of(sb * r_sz + c * c_sz, c_sz), c_sz)
        q = q_ref[rows, :] * (GLA_DK_HEAD ** -0.5)
        k = k_ref[rows, :]
        v = v_ref[rows, :]
        z = _dot(gl_ref[rows, :].astype(BF16), wg) + bg_ref[...]
        la = (jnp.minimum(z, 0.0) - jnp.log(1.0 + jnp.exp(-jnp.abs(z)))) * (1.0 / GLA_GATE_TAU)
        l_hi, l_mid, l_lo = _split3(la)
        b = _dot(tri, l_hi) + (_dot(tri, l_mid) + _dot(tri, l_lo))
        dcol = _dot_tn(l_hi, ones) + (_dot_tn(l_mid, ones) + _dot_tn(l_lo, ones))
        b_last = b[c_sz - 1:c_sz, :]

        s_old = state[sb]
        o = _dot((q * jnp.exp(b)).astype(BF16), s_old.astype(BF16))

        if mxu_intra:
            n_sub = c_sz // GLA_SUB
            parts = [jnp.zeros((GLA_SUB, c_sz), F32)]
            colc = lax.broadcasted_iota(I32, (GLA_SUB, c_sz), 1)
            for blk in range(1, n_sub):
                lo = blk * GLA_SUB
                ref_b = b[lo - 1:lo, :]
                q_rel = q[lo:lo + GLA_SUB, :] * jnp.exp(b[lo:lo + GLA_SUB, :] - ref_b)
                k_rel = k * jnp.exp(jnp.minimum(ref_b - b, 0.0))
                s_blk = _dot_nt(q_rel.astype(BF16), k_rel.astype(BF16))
                parts.append(jnp.where(colc < lo, s_blk, 0.0))
            att = jnp.concatenate(parts, axis=0)

            b3 = b.reshape(n_sub, GLA_SUB, GLA_DK_HEAD)
            q3 = q.reshape(n_sub, GLA_SUB, GLA_DK_HEAD)
            k3 = k.reshape(n_sub, GLA_SUB, GLA_DK_HEAD)
            row3 = lax.broadcasted_iota(I32, b3.shape, 1)
            blk_lane0 = (lax.broadcasted_iota(I32, (c_sz, c_sz), 0) // GLA_SUB) * GLA_SUB
            lane_c = lax.broadcasted_iota(I32, (c_sz, c_sz), 1)
            for jj in range(GLA_SUB):
                e = jnp.exp(jnp.where(row3 >= jj, b3 - b3[:, jj:jj + 1, :], -jnp.inf))
                col = jnp.sum(q3 * e * k3[:, jj:jj + 1, :], axis=-1, keepdims=True).reshape(c_sz, 1)
                att = jnp.where(lane_c == blk_lane0 + jj, col, att)
            o = o + _dot(att.astype(BF16), v.astype(BF16))
        else:
            bsc[sb] = b
            qk_base = sb * r_sz + c * c_sz

            def jbody(j, acc):
                kj = k_ref[pl.ds(qk_base + j, 1), :]
                vj = v_ref[pl.ds(qk_base + j, 1), :]
                e = jnp.exp(jnp.where(rowi >= j, b - bsc[sb, pl.ds(j, 1), :], -jnp.inf))
                return acc + jnp.sum(q * e * kj, axis=-1, keepdims=True) * vj

            o = lax.fori_loop(0, c_sz, jbody, o, unroll=True)

        kt = (k * jnp.exp(b_last - b)).astype(BF16)
        decay = jnp.exp(dcol)
        ds = _dot_tn(kt, v.astype(BF16))
        state[sb] = s_old * jnp.tile(decay, (1, GLA_DV_HEAD // LANES)) + ds

        o = o * lax.rsqrt(jnp.mean(o * o, axis=-1, keepdims=True) + RMS_EPS) * nw_ref[...]
        r = r_ref[rows, :]
        o_ref[rows, :] = o * (r / (1.0 + jnp.exp(-r)))

    for sb in range(seqs):
        if n_chunks == 1:
            one_chunk(sb, 0)
        else:
            def body(c, carry, sb=sb):
                one_chunk(sb, c)
                return carry
            lax.fori_loop(0, n_chunks, body, 0, unroll=4)

    @pl.when(t == pl.num_programs(2) - 1)
    def _():
        sf_ref[...] = state[...]


def _gla(proj, glow, w_gate_pad, b_gate, norm_w, s0, *, n_seq, seq_len, row_start, rows_per_step, chunk,
         mxu_intra):
    r_sz = rows_per_step
    steps = seq_len // r_sz
    seqs = math.gcd(n_seq, GLA_SEQS_PER_STEP) if steps == 1 else 1
    blk_rows = seqs * r_sz
    assert seq_len % r_sz == 0 and r_sz % chunk == 0 and row_start % blk_rows == 0
    rb0 = row_start // blk_rows
    kq, kv = GLA_DK // GLA_DK_HEAD, GLA_DK // GLA_DV_HEAD

    def rowblk(b, t):
        return rb0 + b * steps + t

    st_spec = pl.BlockSpec((seqs, None, GLA_DK_HEAD, GLA_DV_HEAD), lambda b, h, t: (b, h, 0, 0))
    kern = functools.partial(_gla_kernel, chunk=chunk, n_chunks=r_sz // chunk, seqs=seqs, mxu_intra=mxu_intra)
    return pl.pallas_call(
        kern,
        grid=(n_seq // seqs, GLA_HEADS, steps),
        in_specs=[
            pl.BlockSpec((blk_rows, GLA_DK_HEAD), lambda b, h, t: (rowblk(b, t), h)),
            pl.BlockSpec((blk_rows, GLA_DK_HEAD), lambda b, h, t: (rowblk(b, t), kq + h)),
            pl.BlockSpec((blk_rows, GLA_DV_HEAD), lambda b, h, t: (rowblk(b, t), 2 * kv + h)),
            pl.BlockSpec((blk_rows, GLA_DV_HEAD), lambda b, h, t: (rowblk(b, t), 2 * kv + GLA_HEADS + h)),
            pl.BlockSpec((blk_rows, LANES), lambda b, h, t: (rowblk(b, t), 0)),
            pl.BlockSpec((LANES, GLA_DK_HEAD), lambda b, h, t: (0, h)),
            pl.BlockSpec((1, GLA_DK_HEAD), lambda b, h, t: (0, h)),
            pl.BlockSpec((1, GLA_DV_HEAD), lambda b, h, t: (0, 0)),
            st_spec,
        ],
        out_specs=[
            pl.BlockSpec((blk_rows, GLA_DV_HEAD), lambda b, h, t: (b * steps + t, h)),
            st_spec,
        ],
        out_shape=[jax.ShapeDtypeStruct((n_seq * seq_len, GLA_DV), F32),
                   jax.ShapeDtypeStruct((n_seq, GLA_HEADS, GLA_DK_HEAD, GLA_DV_HEAD), F32)],
        scratch_shapes=[pltpu.VMEM((seqs, GLA_DK_HEAD, GLA_DV_HEAD), F32),
                        pltpu.VMEM((seqs, chunk, GLA_DK_HEAD), F32)],
        compiler_params=_params("arbitrary", "arbitrary", "arbitrary"),
        name="gla_chunked",
    )(proj, proj, proj, proj, glow, w_gate_pad, b_gate, norm_w, s0)


def _layer_norm(x, g, b):
    mu = jnp.mean(x, axis=-1, keepdims=True)
    xc = x - mu
    var = jnp.mean(xc * xc, axis=-1, keepdims=True)
    return xc * lax.rsqrt(var + LN_EPS) * g + b


def _route(x, wr_ref, br_ref, idx_ref, gate_ref, cnt_ref, seen):
    tm = x.shape[0]
    xh, xm, xl = _split3(x)
    wh, wm, wl = _split3(wr_ref[...])
    logits = _dot_nt(wh, xh) + ((_dot_nt(wm, xh) + _dot_nt(wh, xm))
                                + ((_dot_nt(wm, xm) + _dot_nt(wl, xh)) + _dot_nt(wh, xl)))
    scores = 1.0 / (1.0 + jnp.exp(-logits))
    lane = lax.broadcasted_iota(I32, (N_EXPERTS, tm), 0).astype(F32)
    sel = scores + br_ref[...]
    big = float(N_EXPERTS)
    best = None
    for g in range(N_EXPERT_GROUPS):
        in_g = (lane >= g * EXPERTS_PER_GROUP) & (lane < (g + 1) * EXPERTS_PER_GROUP)
        vg = jnp.where(in_g, sel, -jnp.inf)
        m1 = jnp.max(vg, axis=0, keepdims=True)
        i1 = jnp.min(jnp.where(vg == m1, lane, big), axis=0, keepdims=True)
        vg2 = jnp.where(lane == i1, -jnp.inf, vg)
        m2 = jnp.max(vg2, axis=0, keepdims=True)
        i2 = jnp.min(jnp.where(vg2 == m2, lane, big), axis=0, keepdims=True)
        gs = m1 + m2
        if best is None:
            best, b1, b2 = gs, i1, i2
        else:
            better = gs > best
            best = jnp.where(better, gs, best)
            b1 = jnp.where(better, i1, b1)
            b2 = jnp.where(better, i2, b2)
    w1 = jnp.sum(jnp.where(lane == b1, scores, 0.0), axis=0, keepdims=True)
    w2 = jnp.sum(jnp.where(lane == b2, scores, 0.0), axis=0, keepdims=True)
    den = w1 + w2
    picked = jnp.logical_or(lane == b1, lane == b2).astype(F32)
    earlier = (lax.broadcasted_iota(I32, (tm, tm), 0) < lax.broadcasted_iota(I32, (tm, tm), 1)).astype(BF16)
    before = _dot(picked.astype(BF16), earlier) + seen[...]
    r1 = jnp.sum(jnp.where(lane == b1, before, 0.0), axis=0, keepdims=True)
    r2 = jnp.sum(jnp.where(lane == b2, before, 0.0), axis=0, keepdims=True)
    seen[...] = seen[...] + jnp.sum(picked, axis=1, keepdims=True)
    cnt_ref[...] = jnp.broadcast_to(seen[...], cnt_ref.shape)
    row = lax.broadcasted_iota(I32, (SUBLANES, tm), 0)
    idx_ref[...] = jnp.where(row == 0, b1, jnp.where(row == 1, b2, jnp.where(row == 2, r1, jnp.where(
        row == 3, r2, 0.0)))).astype(I32)
    gate_ref[...] = jnp.where(row == 0, w1 / den, jnp.where(row == 1, w2 / den, 0.0))


def _ln_route_kernel(*refs, first_tiles):
    *x_refs, y_ref, g_ref, b_ref, wr_ref, br_ref, o_ref, idx_ref, gate_ref, cnt_ref, seen = refs

    @pl.when(pl.program_id(0) == 0)
    def _():
        seen[...] = jnp.zeros(seen.shape, F32)

    x = _segment_tile(x_refs, first_tiles)
    x1 = _layer_norm(DEEPNORM_ALPHA * x + y_ref[...], g_ref[...], b_ref[...])
    o_ref[...] = x1
    _route(x1, wr_ref, br_ref, idx_ref, gate_ref, cnt_ref, seen)


def _ln_route(x, y, g, b, w_router_t, b_router_col, tm):
    segs = _row_segments(x)
    n, d = y.shape
    row = lambda i: (i, 0)
    col = lambda i: (0, i)
    fixed = lambda i: (0, 0)
    x_specs, first = _segment_specs(segs, tm)
    return pl.pallas_call(
        functools.partial(_ln_route_kernel, first_tiles=first),
        grid=(n // tm,),
        in_specs=x_specs + [pl.BlockSpec((tm, d), row),
                  pl.BlockSpec((1, d), fixed), pl.BlockSpec((1, d), fixed),
                  pl.BlockSpec((N_EXPERTS, d), fixed), pl.BlockSpec((N_EXPERTS, 1), fixed)],
        out_specs=[pl.BlockSpec((tm, d), row), pl.BlockSpec((SUBLANES, tm), col), pl.BlockSpec((SUBLANES, tm), col),
                   pl.BlockSpec((N_EXPERTS, LANES), fixed)],
        out_shape=[jax.ShapeDtypeStruct((n, d), F32), jax.ShapeDtypeStruct((SUBLANES, n), I32),
                   jax.ShapeDtypeStruct((SUBLANES, n), F32), jax.ShapeDtypeStruct((N_EXPERTS, LANES), F32)],
        scratch_shapes=[pltpu.VMEM((N_EXPERTS, 1), F32)],
        compiler_params=_params("arbitrary"),
        name="deepnorm_ln_router",
    )(*segs, y, g, b, w_router_t, b_router_col)


def _gmm1_kernel(src_ref, te_ref, ch_ref, nu_ref, x_hbm, wg_ref, wu_ref, h_ref, xbuf, wgb, wub, sems, *, tm):
    i = pl.program_id(0)
    n_used = nu_ref[0]

    def row_copy(tile, r, slot):
        return pltpu.make_async_copy(x_hbm.at[pl.ds(src_ref[tile * tm + r], 1), :],
                                     xbuf.at[slot, pl.ds(r, 1), :], sems.at[slot])

    def fetch(tile, slot):
        def issue(r, c):
            row_copy(tile, r, slot).start()
            return c
        lax.fori_loop(0, tm, issue, 0, unroll=DMA_UNROLL)

    def drain(tile, slot):
        def wait_row(r, c):
            row_copy(tile, r, slot).wait()
            return c
        lax.fori_loop(0, tm, wait_row, 0, unroll=DMA_UNROLL)

    @pl.when(i == 0)
    def _():
        fetch(0, 0)

    @pl.when(i < n_used)
    def _():
        slot = i % 2
        drain(i, slot)

        @pl.when(ch_ref[i] == 1)
        def _():
            wgb[...] = wg_ref[...].astype(BF16)
            wub[...] = wu_ref[...].astype(BF16)

        x = xbuf[slot].astype(BF16)
        a = _dot(x, wgb[...])
        for r in range(tm):
            row_copy(i + 1, r, 1 - slot).start(priority=r % 2)
        u = _dot(x, wub[...])
        h_ref[...] = ((a / (1.0 + jnp.exp(-a))) * u).astype(BF16)

    @pl.when(i >= n_used)
    def _():
        @pl.when(i == n_used)
        def _():
            drain(i, i % 2)

        h_ref[...] = jnp.zeros(h_ref.shape, h_ref.dtype)


def _gmm1(x, src_row, w_gate, w_up, layer, meta, tm):
    te, _, ch, nu = meta
    n_tiles = src_row.shape[0] // tm
    d = x.shape[1]
    f = w_gate.shape[-1]
    wspec = pl.BlockSpec((None, None, d, f), lambda i, src, te, ch, nu: (layer, te[i], 0, 0))
    return pl.pallas_call(
        functools.partial(_gmm1_kernel, tm=tm),
        grid_spec=pltpu.PrefetchScalarGridSpec(
            num_scalar_prefetch=4,
            grid=(n_tiles,),
            in_specs=[pl.BlockSpec(memory_space=pl.ANY), wspec, wspec],
            out_specs=pl.BlockSpec((tm, f), lambda i, src, te, ch, nu: (i, 0)),
            scratch_shapes=[pltpu.VMEM((2, tm, d), F32), pltpu.VMEM((d, f), BF16), pltpu.VMEM((d, f), BF16),
                            pltpu.SemaphoreType.DMA((2,))],
        ),
        out_shape=jax.ShapeDtypeStruct((n_tiles * tm, f), BF16),
        compiler_params=_params("arbitrary"),
        name="moe_gather_gate_up",
    )(src_row, te, ch, nu, x, w_gate, w_up)


def _gmm2_kernel(te_ref, tb_ref, ch_ref, nu_ref, h_ref, wd_ref, y_ref, wdb):
    i = pl.program_id(1)

    @pl.when(i < nu_ref[0])
    def _():
        @pl.when(ch_ref[i] == 1)
        def _():
            wdb[...] = wd_ref[...].astype(BF16)

        y_ref[...] = _dot(h_ref[...], wdb[...])

    @pl.when(i >= nu_ref[0])
    def _():
        y_ref[...] = jnp.zeros(y_ref.shape, y_ref.dtype)


def _gmm2(h, w_down, layer, meta, tm, tn):
    te, tb, ch, nu = meta
    n_tiles = h.shape[0] // tm
    f = h.shape[1]
    d = w_down.shape[-1]
    return pl.pallas_call(
        _gmm2_kernel,
        grid_spec=pltpu.PrefetchScalarGridSpec(
            num_scalar_prefetch=4,
            grid=(d // tn, n_tiles),
            in_specs=[pl.BlockSpec((tm, f), lambda j, i, te, tb, ch, nu: (tb[i], 0)),
                      pl.BlockSpec((None, None, f, tn), lambda j, i, te, tb, ch, nu: (layer, te[i], 0, j))],
            out_specs=pl.BlockSpec((tm, tn), lambda j, i, te, tb, ch, nu: (i, j)),
            scratch_shapes=[pltpu.VMEM((f, tn), BF16)],
        ),
        out_shape=jax.ShapeDtypeStruct((n_tiles * tm, d), F32),
        compiler_params=_params("arbitrary", "arbitrary"),
        name="moe_down",
    )(te, tb, ch, nu, h, w_down)


def _combine_ln_kernel(pos_ref, y_hbm, x_ref, gate_ref, g_ref, b_ref, *rest, tm, n_split_tiles):
    if n_split_tiles is None:
        (o_ref, ybuf, sems) = rest
    else:
        (op_ref, os_ref, ybuf, sems) = rest
    i = pl.program_id(0)

    def row_copy(tile, r, k, slot):
        return pltpu.make_async_copy(y_hbm.at[pl.ds(pos_ref[2 * (tile * tm + r) + k], 1), :],
                                     ybuf.at[slot, k, pl.ds(r, 1), :], sems.at[slot])

    def fetch(tile, slot):
        def issue(r, c):
            row_copy(tile, r, 0, slot).start()
            row_copy(tile, r, 1, slot).start(priority=1)
            return c
        lax.fori_loop(0, tm, issue, 0, unroll=DMA_UNROLL)

    @pl.when(i == 0)
    def _():
        fetch(0, 0)

    @pl.when(i + 1 < pl.num_programs(0))
    def _():
        fetch(i + 1, (i + 1) % 2)

    slot = i % 2

    def drain(r, c):
        row_copy(i, r, 0, slot).wait()
        row_copy(i, r, 1, slot).wait()
        return c

    lax.fori_loop(0, tm, drain, 0, unroll=DMA_UNROLL)
    gate = gate_ref[...]
    moe = gate[:, 0:1] * ybuf[slot, 0] + gate[:, 1:2] * ybuf[slot, 1]
    out = _layer_norm(DEEPNORM_ALPHA * x_ref[...] + moe, g_ref[...], b_ref[...])
    if n_split_tiles is None:
        o_ref[...] = out
    else:
        @pl.when(i < n_split_tiles)
        def _():
            op_ref[...] = out

        @pl.when(i >= n_split_tiles)
        def _():
            os_ref[...] = out


def _combine_ln(y_sorted, pos, x, gate, g, b, tm, n_split=None):
    n, d = x.shape
    row = lambda i, pos: (i, 0)
    fixed = lambda i, pos: (0, 0)
    if n_split is None:
        nst = None
        out_specs = pl.BlockSpec((tm, d), row)
        out_shape = jax.ShapeDtypeStruct((n, d), F32)
    else:
        nst = n_split // tm
        out_specs = [pl.BlockSpec((tm, d), lambda i, pos: (jnp.minimum(i, nst - 1), 0)),
                     pl.BlockSpec((tm, d), lambda i, pos: (jnp.maximum(i - nst, 0), 0))]
        out_shape = [jax.ShapeDtypeStruct((n_split, d), F32), jax.ShapeDtypeStruct((n - n_split, d), F32)]
    return pl.pallas_call(
        functools.partial(_combine_ln_kernel, tm=tm, n_split_tiles=nst),
        grid_spec=pltpu.PrefetchScalarGridSpec(
            num_scalar_prefetch=1,
            grid=(n // tm,),
            in_specs=[pl.BlockSpec(memory_space=pl.ANY), pl.BlockSpec((tm, d), row),
                      pl.BlockSpec((tm, 2), row), pl.BlockSpec((1, d), fixed), pl.BlockSpec((1, d), fixed)],
            out_specs=out_specs,
            scratch_shapes=[pltpu.VMEM((2, 2, tm, d), F32), pltpu.SemaphoreType.DMA((2,))],
        ),
        out_shape=out_shape,
        compiler_params=_params("arbitrary"),
        name="moe_combine_ln",
    )(pos, y_sorted, x, gate, g, b)


def _dispatch_meta(idx, counts, tm, n_tiles):
    n = idx.shape[1]
    e_flat = idx[0:2].T.reshape(-1)
    rank = idx[2:4].T.reshape(-1)
    counts = counts[:, 0].astype(I32)
    ptiles = (counts + tm - 1) // tm
    tile_end = jnp.cumsum(ptiles)
    n_used = tile_end[-1]
    row_off = (tile_end - ptiles) * tm
    pos = row_off[e_flat] + rank
    tok = jnp.arange(2 * n, dtype=I32) // 2
    src_row = jnp.zeros((n_tiles * tm,), I32).at[pos].set(tok)
    tile_id = jnp.minimum(jnp.arange(n_tiles, dtype=I32), n_used - 1)
    tile_expert = jnp.sum((tile_id[:, None] >= tile_end[None, :]).astype(I32), axis=1)
    changed = jnp.concatenate([jnp.ones((1,), I32), (tile_expert[1:] != tile_expert[:-1]).astype(I32)])
    nu = n_used.reshape(1).astype(I32)
    return src_row, pos.astype(I32), (tile_expert, tile_id.astype(I32), changed, nu)


DMA_UNROLL = 8
MOE_TM = 256
MOE_TN = 2048


def _moe_ln(x1, idx, gate, counts, w_gate, w_up, w_down, layer, g, b, n_split=None):
    n = x1.shape[0]
    n_tiles = -(-2 * n // MOE_TM) + N_EXPERTS + 1
    src_row, pos, meta = _dispatch_meta(idx, counts, MOE_TM, n_tiles)
    h = _gmm1(x1, src_row, w_gate, w_up, layer, meta, MOE_TM)
    y = _gmm2(h, w_down, layer, meta, MOE_TM, MOE_TN)
    tm = _pick_tile(n if n_split is None else math.gcd(n, n_split), LN_TM)
    return _combine_ln(y, pos, x1, gate[0:2].T, g, b, tm, n_split)


SWA_BLK = 128
SWA_SUPER = SWA_BLK * 16
SWA_SCALE = HEAD_DIM ** -0.5
SWA_TOEPLITZ_PERIOD = 4 * SWA_BLK


def _t5_bucket(dist):
    max_exact = NUM_BUCKETS // 2
    d = jnp.maximum(dist.astype(F32), 1.0)
    large = max_exact + (jnp.log(d / max_exact) / math.log(MAX_DISTANCE / max_exact)
                         * (NUM_BUCKETS - max_exact)).astype(I32)
    large = jnp.minimum(large, NUM_BUCKETS - 1)
    return jnp.where(dist < max_exact, dist, large)


def _group_biases(rel_bias):
    out = []
    for g, (w, d) in enumerate(DIL_GROUPS):
        dist = d * jnp.arange(w // d + 1, dtype=I32)
        b = rel_bias[_t5_bucket(dist)]
        out.append(b[:, g * SWA_HEADS:(g + 1) * SWA_HEADS].T.astype(F32))
    return out


def _prompt_bias_tables(biases):
    b_all = jnp.concatenate(biases, axis=0)
    mask = jnp.full((b_all.shape[0], SWA_TOEPLITZ_PERIOD - SWA_BLK - 1), NEG_BIG, F32)
    return jnp.concatenate([b_all[:, ::-1] * LOG2_E, mask], axis=1)[:, None, :]


def _swa_prompt_kernel(q0_ref, q1_ref, q2_ref, kc_ref, kp_ref, vc_ref, vp_ref, t0_ref, t1_ref, t2_ref,
                       o_ref, kk, vv, og, ls):
    s = pl.program_id(1)
    sb = SWA_SUPER
    kk[0:sb, :] = kp_ref[...]
    kk[sb:2 * sb, :] = kc_ref[...]
    vv[0:sb, :] = vp_ref[...]
    vv[sb:2 * sb, :] = vc_ref[...]
    col = lax.broadcasted_iota(I32, (SWA_BLK, 2 * SWA_BLK), 1)
    before_start = col < SWA_BLK

    for g, (_, d) in enumerate(DIL_GROUPS):
        q_ref = (q0_ref, q1_ref, q2_ref)[g]
        u_row = jnp.broadcast_to((t0_ref, t1_ref, t2_ref)[g][...], (SWA_BLK, SWA_TOEPLITZ_PERIOD))
        tab = pltpu.roll(u_row, 0, 1, stride=1, stride_axis=0)[:, :2 * SWA_BLK]
        n_mb = sb // (SWA_BLK * d)

        def block(idx, carry, q_ref=q_ref, tab=tab, d=d, n_mb=n_mb, g=g):
            r = idx // n_mb
            mb = idx % n_mb
            start = r + d * SWA_BLK * mb
            kstart = sb + start - d * SWA_BLK
            if d == 1:
                rows_q, rows_k = pl.ds(start, SWA_BLK), pl.ds(kstart, 2 * SWA_BLK)
            else:
                rows_q, rows_k = pl.ds(start, SWA_BLK, stride=d), pl.ds(kstart, 2 * SWA_BLK, stride=d)
            q = q_ref[rows_q, :].astype(BF16)
            k = kk[rows_k, :].astype(BF16)
            v = vv[rows_k, :].astype(BF16)
            lg = _dot_nt(q, k) * (SWA_SCALE * LOG2_E) + tab
            no_prev = jnp.logical_and(s == 0, mb == 0)
            lg = jnp.where(jnp.logical_and(no_prev, before_start), NEG_BIG, lg)
            m = jnp.max(lg, axis=-1, keepdims=True)
            p = jnp.exp2(lg - m)
            l = jnp.sum(p, axis=-1, keepdims=True)
            og[g, rows_q, :] = _dot(p.astype(BF16), v) / l
            ls[g, rows_q, :] = jnp.broadcast_to(m + jnp.log2(l), (SWA_BLK, HEAD_DIM))
            return carry

        lax.fori_loop(0, sb // SWA_BLK, block, 0, unroll=True)

    l0, l1, l2 = ls[0], ls[1], ls[2]
    mx = jnp.maximum(jnp.maximum(l0, l1), l2)
    e0, e1, e2 = jnp.exp2(l0 - mx), jnp.exp2(l1 - mx), jnp.exp2(l2 - mx)
    o_ref[...] = (e0 * og[0] + e1 * og[1] + e2 * og[2]) / (e0 + e1 + e2)


def _swa_prompt(proj, tables, n_p):
    sb = SWA_SUPER
    assert n_p % sb == 0
    kcol, vcol = SWA_NQ // HEAD_DIM, (SWA_NQ + SWA_NKV) // HEAD_DIM
    blk = lambda f: pl.BlockSpec((sb, HEAD_DIM), f)
    tspec = lambda g: pl.BlockSpec((None, 1, SWA_TOEPLITZ_PERIOD), lambda h, s: (g * SWA_HEADS + h, 0, 0))
    prev = lambda s: jnp.maximum(s - 1, 0)
    return pl.pallas_call(
        _swa_prompt_kernel,
        grid=(SWA_HEADS, n_p // sb),
        in_specs=[blk(lambda h, s: (s, h)), blk(lambda h, s: (s, SWA_HEADS + h)), blk(lambda h, s: (s, 2 * SWA_HEADS + h)),
                  blk(lambda h, s: (s, kcol + h)), blk(lambda h, s: (prev(s), kcol + h)),
                  blk(lambda h, s: (s, vcol + h)), blk(lambda h, s: (prev(s), vcol + h)),
                  tspec(0), tspec(1), tspec(2)],
        out_specs=blk(lambda h, s: (s, h)),
        out_shape=jax.ShapeDtypeStruct((n_p, SWA_NKV), F32),
        scratch_shapes=[pltpu.VMEM((2 * sb, HEAD_DIM), F32), pltpu.VMEM((2 * sb, HEAD_DIM), F32),
                        pltpu.VMEM((N_DIL, sb, HEAD_DIM), F32), pltpu.VMEM((N_DIL, sb, HEAD_DIM), F32)],
        compiler_params=_params("arbitrary", "arbitrary"),
        name="swa_prompt",
    )(proj, proj, proj, proj, proj, proj, proj, tables, tables, tables)


SWA_MB = 32
SWA_RES = 16
SWA_KEYS_PER_PASS = 16
LOG2_E = math.log2(math.e)


def _sample_bias_tables(biases, l_cache, t_s):
    b_near, b_mid, b_wide = biases
    n_groups = l_cache // SWA_RES
    neg = jnp.full((SWA_HEADS, 1), NEG_BIG, F32)

    def pick(b, j):
        j = np.asarray(j)
        ok = (j >= 0) & (j <= SWA_BLK)
        vals = jnp.concatenate([b, neg], axis=1)[:, np.where(ok, j, SWA_BLK + 1)]
        return jnp.moveaxis(vals, 0, -1)

    t_wide = pick(b_wide, n_groups - np.arange(n_groups))
    variant = np.arange(5)[:, None]
    t_mid = pick(b_mid, (SWA_BLK - 3 + variant) - 4 * np.arange(SWA_MB)[None, :])
    t_near = pick(b_near, SWA_BLK + 7 - np.arange(SWA_BLK + 7))
    d_new = np.arange(t_s)[:, None] - np.arange(t_s)[None, :]
    t_new = jnp.stack([pick(b, np.where((d_new >= 0) & (d_new % d == 0), d_new // d, -1))
                       for b, (_, d) in zip(biases, DIL_GROUPS)], axis=0)
    rep = lambda t: jnp.broadcast_to((t * LOG2_E)[..., None], t.shape + (HEAD_DIM,))
    return rep(t_wide), rep(t_mid), rep(t_near), rep(t_new)


def _swa_sample_kernel(q_ref, kn_ref, vn_ref, ka_ref, va_ref, kl_ref, vl_ref, kb_ref, vb_ref,
                       tw_ref, tm_ref, tnr_ref, tnew_ref, o_ref, m_s, l_s, acc, *, t_s, n_groups):
    c = pl.program_id(1)
    n_steps = pl.num_programs(1)
    mb = SWA_MB

    @pl.when(c == 0)
    def _():
        m_s[...] = jnp.full(m_s.shape, NEG_BIG, F32)
        l_s[...] = jnp.zeros(l_s.shape, F32)
        acc[...] = jnp.zeros(acc.shape, F32)

    def absorb(slot, q, kt, vt, bias):
        m_run, l_run, a_run = m_s[slot], l_s[slot], acc[slot]
        for lo in range(0, kt.shape[0], SWA_KEYS_PER_PASS):
            hi = min(lo + SWA_KEYS_PER_PASS, kt.shape[0])
            s = jnp.sum(q[None] * kt[lo:hi], axis=-1, keepdims=True) + bias[lo:hi]
            m_new = jnp.maximum(m_run, jnp.max(s, axis=0))
            alpha = jnp.exp2(m_run - m_new)
            p = jnp.exp2(s - m_new[None])
            l_run = alpha * l_run + jnp.sum(p, axis=0)
            a_run = alpha * a_run + jnp.sum(p * vt[lo:hi], axis=0)
            m_run = m_new
        m_s[slot], l_s[slot], acc[slot] = m_run, l_run, a_run

    def query(i, g):
        return q_ref[i, g] * (SWA_SCALE * LOG2_E)

    def wide(i, carry):
        absorb(2 * t_s + i, query(i, 2), ka_ref[:, i], va_ref[:, i], tw_ref[pl.ds(c * mb, mb)])
        return carry

    lax.fori_loop(0, t_s, wide, 0, unroll=True)

    @pl.when(c == 0)
    def _():
        def wide_last(i, carry):
            absorb(2 * t_s + i, query(i, 2), kl_ref[:, i], vl_ref[:, i], tw_ref[n_groups - mb:n_groups])
            return carry

        lax.fori_loop(0, t_s, wide_last, 0, unroll=True)

        def fresh(i, carry):
            for g in range(N_DIL):
                absorb(g * t_s + i, query(i, g), kn_ref[...], vn_ref[...], tnew_ref[g, i])
            return carry

        lax.fori_loop(0, t_s, fresh, 0, unroll=2)

    @pl.when(c == 1)
    def _():
        def mid(i, carry):
            rho, hi = i % 4, i // 4
            kts, vts, bs = [], [], []
            for rr in range(4):
                k_src, v_src = (kl_ref, vl_ref) if rr < 2 else (kb_ref, vb_ref)
                r_idx = rho + 4 * (rr % 2)
                kts.append(k_src[:, r_idx])
                vts.append(v_src[:, r_idx])
                bs.append(tm_ref[hi - rr + 3])
            absorb(t_s + i, query(i, 1), jnp.concatenate(kts, axis=0), jnp.concatenate(vts, axis=0),
                   jnp.concatenate(bs, axis=0))
            return carry

        lax.fori_loop(0, t_s, mid, 0, unroll=2)

    @pl.when(c == n_steps - 1)
    def _():
        near_groups = SWA_BLK // SWA_RES
        n_near = near_groups * (SWA_RES // 2)

        def near(i, carry):
            kts, vts, bs = [], [], []
            for k_src, v_src, off in ((kl_ref, vl_ref, 7), (kb_ref, vb_ref, 7 + SWA_RES // 2)):
                kts.append(k_src[mb - near_groups:mb].reshape(n_near, SWA_HEADS, HEAD_DIM))
                vts.append(v_src[mb - near_groups:mb].reshape(n_near, SWA_HEADS, HEAD_DIM))
                bs += [tnr_ref[pl.ds(SWA_RES * grp + off - i, SWA_RES // 2)] for grp in range(near_groups)]
            absorb(i, query(i, 0), jnp.concatenate(kts, axis=0), jnp.concatenate(vts, axis=0),
                   jnp.concatenate(bs, axis=0))
            return carry

        lax.fori_loop(0, t_s, near, 0, unroll=2)

        for i in range(t_s):
            lse = [m_s[g * t_s + i] + jnp.log2(l_s[g * t_s + i]) for g in range(N_DIL)]
            mx = jnp.maximum(jnp.maximum(lse[0], lse[1]), lse[2])
            e = [jnp.exp2(x - mx) for x in lse]
            num = sum(e[g] * (acc[g * t_s + i] / l_s[g * t_s + i]) for g in range(N_DIL))
            o_ref[i] = num / (e[0] + e[1] + e[2])


def _swa_sample(q_new, k_new, v_new, cache_k, cache_v, tables, n_b, t_s):
    l_cache = cache_k.shape[1]
    n_groups = l_cache // SWA_RES
    n_steps = n_groups // SWA_MB - 1
    assert t_s == SWA_RES // 2 and l_cache % (SWA_RES * SWA_MB) == 0 and l_cache >= SWA_MAX_WINDOW and n_steps >= 3
    ck = cache_k.reshape(n_b, n_groups, SWA_RES, SWA_HEADS, HEAD_DIM)
    cv = cache_v.reshape(n_b, n_groups, SWA_RES, SWA_HEADS, HEAD_DIM)
    half = (SWA_MB, SWA_RES // 2, SWA_HEADS, HEAD_DIM)
    a_spec = pl.BlockSpec((None,) + half, lambda b, c: (b, c, 0, 0, 0))
    last_spec = lambda r: pl.BlockSpec((None,) + half, lambda b, c: (b, n_steps, r, 0, 0))
    new_spec = pl.BlockSpec((None, t_s, SWA_HEADS, HEAD_DIM), lambda b, c: (b, 0, 0, 0))
    const = lambda t: pl.BlockSpec(t.shape, lambda b, c: (0,) * t.ndim)
    slots = N_DIL * t_s
    state = pltpu.VMEM((slots, SWA_HEADS, HEAD_DIM), F32)
    return pl.pallas_call(
        functools.partial(_swa_sample_kernel, t_s=t_s, n_groups=n_groups),
        grid=(n_b, n_steps),
        in_specs=[pl.BlockSpec((None, t_s, N_DIL, SWA_HEADS, HEAD_DIM), lambda b, c: (b, 0, 0, 0, 0)),
                  new_spec, new_spec, a_spec, a_spec, last_spec(0), last_spec(0), last_spec(1), last_spec(1)]
                 + [const(t) for t in tables],
        out_specs=new_spec,
        out_shape=jax.ShapeDtypeStruct((n_b, t_s, SWA_HEADS, HEAD_DIM), F32),
        scratch_shapes=[state, state, state],
        compiler_params=_params("arbitrary", "arbitrary"),
        name="swa_sample",
    )(q_new, k_new, v_new, ck, cv, ck, cv, ck, cv, *tables)


def _swa_mixer(x, n_p, n_b, t_s, cache_k, cache_v, w_in, w_out, rel_bias):
    tm = _pick_tile(math.gcd(n_p, n_b * t_s), PROJ_TM)
    proj = _matmul(x, w_in, SWA_NQ + 2 * SWA_NKV, tm, 1024)
    biases = _group_biases(rel_bias)
    o_p = _swa_prompt(proj, _prompt_bias_tables(biases), n_p)
    k_all = proj[:, SWA_NQ:SWA_NQ + SWA_NKV]
    v_all = proj[:, SWA_NQ + SWA_NKV:]
    per_head = (n_b, t_s, SWA_HEADS, HEAD_DIM)
    o_s = _swa_sample(proj[n_p:, :SWA_NQ].reshape(n_b, t_s, N_DIL, SWA_HEADS, HEAD_DIM),
                      k_all[n_p:].reshape(per_head), v_all[n_p:].reshape(per_head), cache_k, cache_v,
                      _sample_bias_tables(biases, cache_k.shape[1], t_s), n_b, t_s)
    y = _matmul((o_p, o_s.reshape(n_b * t_s, SWA_NKV)), w_out, D_MODEL, tm, 512)
    return y, k_all, v_all


def _pad_rows(w, rows):
    return jnp.zeros((rows, w.shape[1]), w.dtype).at[:w.shape[0]].set(w)


def _pad_cols(w, cols):
    return jnp.zeros((w.shape[0], cols), w.dtype).at[:, :w.shape[1]].set(w)


def _gla_mixer(x0, n_p, n_b, t_s, state0, w_in, w_gate, b_gate, norm_w, w_out):
    tm = _pick_tile(math.gcd(n_p, n_b * t_s), PROJ_TM)
    proj = _matmul(x0, w_in, GLA_MAIN, tm, 512)
    glow = _matmul(x0, _pad_cols(w_in[:, GLA_MAIN:], LANES), LANES, tm, LANES)
    wg_pad = _pad_rows(w_gate, LANES)
    bg = b_gate.reshape(1, GLA_DK)
    nw = norm_w.reshape(1, GLA_DV_HEAD)
    zero_state = jnp.zeros((1, GLA_HEADS, GLA_DK_HEAD, GLA_DV_HEAD), F32)
    o_p, st_p = _gla(proj, glow, wg_pad, bg, nw, zero_state, n_seq=1, seq_len=n_p, row_start=0,
                     rows_per_step=512, chunk=GLA_CHUNK, mxu_intra=True)
    o_s, st_s = _gla(proj, glow, wg_pad, bg, nw, state0, n_seq=n_b, seq_len=t_s, row_start=n_p,
                     rows_per_step=t_s, chunk=math.gcd(t_s, GLA_CHUNK), mxu_intra=False)
    return _matmul((o_p, o_s), w_out, D_MODEL, tm, 512), st_p, st_s


def kernel(x_prompt, x_sample, state_gla, cache_swa_k, cache_swa_v, gla_w_in, gla_w_gate, gla_b_gate, gla_norm_w,
           gla_w_out, swa_w_in, swa_w_out, rel_bias, w_router, b_router, moe_w_gate, moe_w_up, moe_w_down,
           ln1_g, ln1_b, ln2_g, ln2_b):
    n_p = x_prompt.shape[0] * x_prompt.shape[1]
    n_b, t_s = x_sample.shape[0], x_sample.shape[1]
    n_s = n_b * t_s
    assert x_prompt.shape[0] == 1, "one prompt sequence"
    x = (x_prompt.reshape(n_p, D_MODEL), x_sample.reshape(n_s, D_MODEL))
    w_router_t = w_router.T
    b_router_col = b_router.reshape(N_EXPERTS, 1)
    tm_ln = _pick_tile(math.gcd(n_p, n_s), LN_TM)

    gla_p, gla_s, kp, vp, ksm, vsm = [], [], [], [], [], []
    for i in range(DEPTH):
        j = i // 2
        if i % 2 == 0:
            y, st_p, st_s = _gla_mixer(x, n_p, n_b, t_s, state_gla[j], gla_w_in[j], gla_w_gate[j], gla_b_gate[j],
                                       gla_norm_w[j], gla_w_out[j])
            gla_p.append(st_p)
            gla_s.append(st_s)
        else:
            y, k_all, v_all = _swa_mixer(x, n_p, n_b, t_s, cache_swa_k[j], cache_swa_v[j], swa_w_in[j], swa_w_out[j],
                                         rel_bias)
            keep = min(SWA_MAX_WINDOW, n_p)
            kp.append(k_all[n_p - keep:n_p].reshape(1, keep, SWA_HEADS, HEAD_DIM))
            vp.append(v_all[n_p - keep:n_p].reshape(1, keep, SWA_HEADS, HEAD_DIM))
            ksm.append(k_all[n_p:].reshape(n_b, t_s, SWA_HEADS, HEAD_DIM))
            vsm.append(v_all[n_p:].reshape(n_b, t_s, SWA_HEADS, HEAD_DIM))
        x1, idx, gate, counts = _ln_route(x, y, ln1_g[i:i + 1], ln1_b[i:i + 1], w_router_t, b_router_col, tm_ln)
        last = i == DEPTH - 1
        x = _moe_ln(x1, idx, gate, counts, moe_w_gate, moe_w_up, moe_w_down, i, ln2_g[i:i + 1], ln2_b[i:i + 1],
                    n_split=n_p if last else None)
    y_p, y_s = x
    return (y_p.reshape(x_prompt.shape), y_s.reshape(x_sample.shape), jnp.stack(gla_p), jnp.stack(gla_s),
            jnp.stack(kp), jnp.stack(vp), jnp.stack(ksm), jnp.stack(vsm))
```
